```python
import jax, jax.numpy as jnp
from jax import lax
import numpy as np

D_MODEL = 1024
BATCH = 32
SEQ = 2048
DEPTH = 4

M_HEADS = 4
M_HEAD_DIM = 512
M_WIDTH = M_HEADS * M_HEAD_DIM
M_CHUNK = 64
CONV_WIDTH = 4
M_FORGET_BIAS_LO = 3.0
M_FORGET_BIAS_HI = 6.0
H_HEADS = 8
H_EXPAND = 128
H_VDIM = D_MODEL // H_HEADS
H_KWIDTH = H_HEADS * H_EXPAND
H_VWIDTH = H_HEADS * H_VDIM
H_CHUNK = 16
FFN_HIDDEN = -(-(8 * D_MODEL) // (3 * 256)) * 256
ALPHA = (2 * DEPTH) ** 0.25
BETA = (8 * DEPTH) ** -0.25
LN_EPS = 1e-5
HEAD_NORM_EPS = 1e-6
IN_SIZES = (M_WIDTH, M_WIDTH, M_WIDTH, M_WIDTH, M_HEADS, M_HEADS,
            H_KWIDTH, H_KWIDTH, H_VWIDTH, H_VWIDTH, D_MODEL, D_MODEL)
IN_WIDTH = sum(IN_SIZES)

kernel_name = 'hybrid_mlstm_hgrn2_deepnorm'


def _in_offsets():
    return [int(o) for o in np.cumsum((0,) + IN_SIZES)]


def _split_heads(a, n_heads):
    return a.reshape(a.shape[:2] + (n_heads, -1))


def _to_chunks(a, chunk):
    b, s, h = a.shape[:3]
    a = a.reshape((b, s // chunk, chunk, h) + a.shape[3:])
    return jnp.moveaxis(a, (1, 3), (0, 2))


def _from_chunks(a):
    a = jnp.moveaxis(a, (0, 2), (1, 3))
    return a.reshape((a.shape[0], a.shape[1] * a.shape[2]) + a.shape[3:])


def _layernorm(x, g, b):
    xf = x.astype(jnp.float32)
    mu = jnp.mean(xf, axis=-1, keepdims=True)
    var = jnp.mean(jnp.square(xf - mu), axis=-1, keepdims=True)
    return ((xf - mu) * lax.rsqrt(var + LN_EPS) * g + b).astype(x.dtype)


def _head_layernorm(h, gain):
    mu = jnp.mean(h, axis=-1, keepdims=True)
    var = jnp.mean(jnp.square(h - mu), axis=-1, keepdims=True)
    hn = (h - mu) * lax.rsqrt(var + HEAD_NORM_EPS)
    return hn.reshape(h.shape[:2] + (-1,)) * gain


def _head_rmsnorm(h, gain):
    hn = h * lax.rsqrt(jnp.mean(jnp.square(h), axis=-1, keepdims=True) + HEAD_NORM_EPS)
    return hn.reshape(h.shape[:2] + (-1,)) * gain


def _causal_depthwise_conv(u, w, b):
    c = u.shape[-1]
    y = lax.conv_general_dilated(u, w[:, None, :].astype(u.dtype), window_strides=(1,),
                                 padding=[(CONV_WIDTH - 1, 0)],
                                 dimension_numbers=('NWC', 'WIO', 'NWC'),
                                 feature_group_count=c)
    return y + b


def _hgrn_lower_bounds(lb_logits):
    p = jax.nn.softmax(lb_logits.astype(jnp.float32), axis=0)
    c = jnp.cumsum(p, axis=0)
    return c - c[0:1]


def _mlstm_chunkwise(q, k, v, i_pre, f_pre):
    f32 = jnp.float32
    bsz, _, n_heads, dh = q.shape
    q = q.astype(f32)
    k = k.astype(f32) * (dh ** -0.5)
    v = v.astype(f32)
    log_f = jax.nn.log_sigmoid(f_pre.astype(f32))
    log_i = i_pre.astype(f32)
    causal = jnp.tril(jnp.ones((M_CHUNK, M_CHUNK), dtype=bool))

    def step(carry, xs):
        c_mat, n_vec, m_prev = carry
        qc, kc, vc, ic, lfc = xs
        b = jnp.cumsum(lfc, axis=-1)
        log_w = jnp.where(causal, b[..., :, None] - b[..., None, :] + ic[..., None, :], -jnp.inf)
        log_inter = b + m_prev[..., None]
        m_t = jnp.maximum(log_inter, jnp.max(log_w, axis=-1))
        w = jnp.exp(log_w - m_t[..., None]) * jnp.einsum('bhtd,bhsd->bhts', qc, kc)
        s_inter = jnp.exp(log_inter - m_t)
        num = (jnp.einsum('bhts,bhsd->bhtd', w, vc)
               + s_inter[..., None] * jnp.einsum('bhtk,bhkv->bhtv', qc, c_mat))
        den = jnp.sum(w, axis=-1) + s_inter * jnp.einsum('bhtk,bhk->bht', qc, n_vec)
        h = num / jnp.maximum(jnp.abs(den), jnp.exp(-m_t))[..., None]
        b_end = b[..., -1]
        log_g = b_end[..., None] - b + ic
        m_new = jnp.maximum(b_end + m_prev, jnp.max(log_g, axis=-1))
        decay = jnp.exp(b_end + m_prev - m_new)
        g = jnp.exp(log_g - m_new[..., None])
        c_mat = decay[..., None, None] * c_mat + jnp.einsum('bhs,bhsk,bhsv->bhkv', g, kc, vc)
        n_vec = decay[..., None] * n_vec + jnp.einsum('bhs,bhsk->bhk', g, kc)
        return (c_mat, n_vec, m_new), h

    init = (jnp.zeros((bsz, n_heads, dh, dh), f32),
            jnp.zeros((bsz, n_heads, dh), f32),
            jnp.zeros((bsz, n_heads), f32))
    xs = (_to_chunks(q, M_CHUNK), _to_chunks(k, M_CHUNK), _to_chunks(v, M_CHUNK),
          _to_chunks(log_i, M_CHUNK), _to_chunks(log_f, M_CHUNK))
    _, h = lax.scan(step, init, xs)
    return _from_chunks(h)


def _hgrn2_chunkwise(q, k, v, log_f):
    f32 = jnp.float32
    bsz, _, n_heads, kd = q.shape
    vd = v.shape[-1]
    causal = jnp.tril(jnp.ones((H_CHUNK, H_CHUNK), dtype=bool))[:, :, None]

    def step(state, xs):
        qc, kc, vc, lfc = xs
        a = jnp.cumsum(lfc, axis=2)
        rel = jnp.where(causal, a[:, :, :, None, :] - a[:, :, None, :, :], -jnp.inf)
        scores = jnp.einsum('bhtk,bhtsk,bhsk->bhts', qc, jnp.exp(rel), kc)
        o = (jnp.einsum('bhts,bhsv->bhtv', scores, vc)
             + jnp.einsum('bhtk,bhkv->bhtv', qc * jnp.exp(a), state))
        a_end = a[:, :, -1:, :]
        state = (jnp.exp(a_end[:, :, 0, :])[..., None] * state
                 + jnp.einsum('bhsk,bhsv->bhkv', kc * jnp.exp(a_end - a), vc))
        return state, o

    init = jnp.zeros((bsz, n_heads, kd, vd), f32)
    xs = tuple(_to_chunks(t.astype(f32), H_CHUNK) for t in (q, k, v, log_f))
    _, o = lax.scan(step, init, xs)
    return _from_chunks(o)


def setup_inputs(seed: int = 0) -> dict:
    key = jax.random.key(seed)
    ks = jax.random.split(key, 18)
    f32 = jnp.float32
    off = _in_offsets()

    def nrm(k, shape, scale):
        return jax.random.normal(k, shape, f32) * scale

    x = nrm(ks[0], (BATCH, SEQ, D_MODEL), 1.0)
    w_in = nrm(ks[1], (DEPTH, D_MODEL, IN_WIDTH), D_MODEL ** -0.5)
    w_in = w_in.at[:, :, off[2]:off[3]].multiply(BETA)
    w_in = w_in.at[:, :, off[8]:off[9]].multiply(BETA)
    b_in = nrm(ks[2], (DEPTH, IN_WIDTH), 0.01)
    b_in = b_in.at[:, off[5]:off[6]].add(
        jnp.linspace(M_FORGET_BIAS_LO, M_FORGET_BIAS_HI, M_HEADS, dtype=f32))
    conv_w = nrm(ks[3], (DEPTH, CONV_WIDTH, 2 * M_WIDTH), CONV_WIDTH ** -0.5)
    conv_b = nrm(ks[4], (DEPTH, 2 * M_WIDTH), 0.01)
    m_norm_g = 1.0 + nrm(ks[5], (DEPTH, M_WIDTH), 0.02)
    lb_logits = nrm(ks[6], (DEPTH, H_KWIDTH), 0.1)
    h_norm_g = 1.0 + nrm(ks[7], (DEPTH, H_VWIDTH), 0.02)
    w_proj_a = nrm(ks[8], (DEPTH, M_WIDTH, D_MODEL), BETA * M_WIDTH ** -0.5)
    w_proj_b = nrm(ks[9], (DEPTH, H_VWIDTH, D_MODEL), BETA * H_VWIDTH ** -0.5)
    w_out = nrm(ks[10], (DEPTH, D_MODEL, D_MODEL), BETA * D_MODEL ** -0.5)
    ln1_g = 1.0 + nrm(ks[11], (DEPTH, D_MODEL), 0.02)
    ln1_b = nrm(ks[12], (DEPTH, D_MODEL), 0.02)
    w_ffn_gate = nrm(ks[13], (DEPTH, D_MODEL, FFN_HIDDEN), D_MODEL ** -0.5)
    w_ffn_up = nrm(ks[14], (DEPTH, D_MODEL, FFN_HIDDEN), BETA * D_MODEL ** -0.5)
    w_ffn_down = nrm(ks[15], (DEPTH, FFN_HIDDEN, D_MODEL), BETA * FFN_HIDDEN ** -0.5)
    ln2_g = 1.0 + nrm(ks[16], (DEPTH, D_MODEL), 0.02)
    ln2_b = nrm(ks[17], (DEPTH, D_MODEL), 0.02)
    return {'x': x, 'w_in': w_in, 'b_in': b_in, 'conv_w': conv_w, 'conv_b': conv_b,
            'm_norm_g': m_norm_g, 'lb_logits': lb_logits, 'h_norm_g': h_norm_g,
            'w_proj_a': w_proj_a, 'w_proj_b': w_proj_b, 'w_out': w_out,
            'ln1_g': ln1_g, 'ln1_b': ln1_b, 'w_ffn_gate': w_ffn_gate, 'w_ffn_up': w_ffn_up,
            'w_ffn_down': w_ffn_down, 'ln2_g': ln2_g, 'ln2_b': ln2_b}


def reference(x, w_in, b_in, conv_w, conv_b, m_norm_g, lb_logits, h_norm_g, w_proj_a, w_proj_b,
              w_out, ln1_g, ln1_b, w_ffn_gate, w_ffn_up, w_ffn_down, ln2_g, ln2_b):
    split_points = _in_offsets()[1:-1]
    lower_bounds = _hgrn_lower_bounds(lb_logits)
    for layer in range(DEPTH):
        u = x @ w_in[layer] + b_in[layer]
        (mq, mk, mv, mo, mi, mf, hq, hf, hi, hg, ga, gb) = jnp.split(u, split_points, axis=-1)

        qk = jax.nn.silu(_causal_depthwise_conv(jnp.concatenate([mq, mk], axis=-1),
                                                conv_w[layer], conv_b[layer]))
        mq, mk = jnp.split(qk, 2, axis=-1)
        h_a = _mlstm_chunkwise(_split_heads(mq, M_HEADS), _split_heads(mk, M_HEADS),
                               _split_heads(mv, M_HEADS), mi, mf)
        h_a = (_head_layernorm(h_a, m_norm_g[layer]) * jax.nn.sigmoid(mo)).astype(x.dtype)

        lb = lower_bounds[layer]
        log_f = jnp.logaddexp(jnp.log(lb), jnp.log1p(-lb) + jax.nn.log_sigmoid(hf.astype(jnp.float32)))
        h_b = _hgrn2_chunkwise(_split_heads(jax.nn.silu(hq), H_HEADS),
                               _split_heads(-jnp.expm1(log_f), H_HEADS),
                               _split_heads(hi, H_HEADS),
                               _split_heads(log_f, H_HEADS))
        h_b = (_head_rmsnorm(h_b, h_norm_g[layer]) * jax.nn.sigmoid(hg)).astype(x.dtype)

        mixed = (jax.nn.sigmoid(ga) * (h_a @ w_proj_a[layer])
                 + jax.nn.sigmoid(gb) * (h_b @ w_proj_b[layer]))
        x = _layernorm(ALPHA * x + mixed @ w_out[layer], ln1_g[layer], ln1_b[layer])

        ffn = (jax.nn.silu(x @ w_ffn_gate[layer]) * (x @ w_ffn_up[layer])) @ w_ffn_down[layer]
        x = _layernorm(ALPHA * x + ffn, ln2_g[layer], ln2_b[layer])
    return x
```

```python
import functools

import numpy as np
import jax
import jax.numpy as jnp
from jax import lax
from jax.experimental import pallas as pl
from jax.experimental.pallas import tpu as pltpu

F32 = jnp.float32
BF16 = jnp.bfloat16

M_HEADS = 4
M_HEAD_DIM = 512
M_WIDTH = M_HEADS * M_HEAD_DIM
CONV_WIDTH = 4
H_HEADS = 8
H_DIM = 128
H_WIDTH = H_HEADS * H_DIM
LN_EPS = 1e-5
HEAD_NORM_EPS = 1e-6

SUBLANES = 8
LANES = 128
VMEM_LIMIT_BYTES = 56 * 1024 * 1024

M_CHUNK = 256
H_CHUNK = 64
PROJ_TM = 1024
PROJ_TN = 1024
MERGE_TM = 512
FFN_TM = 512

UB_HQ, UB_HI, UB_HG, UB_GA, UB_GB = 8, 9, 10, 11, 12
UB_WIDTH = 13 * 1024
UF_WIDTH = H_WIDTH + LANES


def _sigmoid(x):
    return 1.0 / (1.0 + jnp.exp(-x))


def _silu(x):
    return x * _sigmoid(x)


def _log_sigmoid(x):
    return jnp.minimum(x, 0.0) - jnp.log1p(jnp.exp(-jnp.abs(x)))


def _split_bf16(x):
    hi = x.astype(BF16)
    lo = (x - hi.astype(F32)).astype(BF16)
    return hi, lo


def _layernorm_rows(y, g, b):
    mu = jnp.mean(y, axis=-1, keepdims=True)
    d = y - mu
    var = jnp.mean(d * d, axis=-1, keepdims=True)
    return d * lax.rsqrt(var + LN_EPS) * g + b


def _params(*semantics):
    return pltpu.CompilerParams(dimension_semantics=semantics, vmem_limit_bytes=VMEM_LIMIT_BYTES)


def _proj_kernel(x_ref, w_ref, b_ref, o_ref):
    acc = jnp.dot(x_ref[...], w_ref[...], preferred_element_type=F32)
    o_ref[...] = (acc + b_ref[...]).astype(o_ref.dtype)


def _proj(xb, w, b, out_dtype, tn):
    t, d = xb.shape
    n = w.shape[1]
    tm = min(PROJ_TM, t)
    return pl.pallas_call(
        _proj_kernel,
        grid=(t // tm, n // tn),
        in_specs=[pl.BlockSpec((tm, d), lambda i, j: (i, 0)),
                  pl.BlockSpec((d, tn), lambda i, j: (0, j)),
                  pl.BlockSpec((1, tn), lambda i, j: (0, j))],
        out_specs=pl.BlockSpec((tm, tn), lambda i, j: (i, j)),
        out_shape=jax.ShapeDtypeStruct((t, n), out_dtype),
        compiler_params=_params("parallel", "arbitrary"),
        name="in_proj",
    )(xb, w, b)


def _mlstm_kernel(q_ref, k_ref, v_ref, og_ref, gcol_ref, grow_ref, cw_ref, cb_ref, ng_ref,
                  o_ref, c_ref, n_ref, m_ref, tail_ref):
    L = q_ref.shape[0]
    dh = M_HEAD_DIM

    @pl.when(pl.program_id(1) == 0)
    def _():
        c_ref[...] = jnp.zeros_like(c_ref)
        n_ref[...] = jnp.zeros_like(n_ref)
        m_ref[...] = jnp.zeros_like(m_ref)
        tail_ref[...] = jnp.zeros_like(tail_ref)

    row = lax.broadcasted_iota(jnp.int32, (L, L), 0)
    col = lax.broadcasted_iota(jnp.int32, (L, L), 1)
    causal = col <= row
    tri = causal.astype(BF16)
    tri_t = (row <= col).astype(BF16)

    gcol = gcol_ref[...]
    grow = grow_ref[...]
    lf_hi, lf_lo = _split_bf16(_log_sigmoid(gcol))
    b_col = (jnp.dot(tri, lf_hi, preferred_element_type=F32)
             + jnp.dot(tri, lf_lo, preferred_element_type=F32))
    lr_hi, lr_lo = _split_bf16(_log_sigmoid(grow))
    b_row = (jnp.dot(lr_hi, tri_t, preferred_element_type=F32)
             + jnp.dot(lr_lo, tri_t, preferred_element_type=F32))

    scale = dh ** -0.5
    for h in range(M_HEADS):
        sl = slice(h * dh, (h + 1) * dh)
        slk = slice(M_WIDTH + h * dh, M_WIDTH + (h + 1) * dh)

        def conv_silu(pre, tail, w, bias):
            xe = jnp.concatenate([tail, pre], axis=0)
            acc = pre * w[CONV_WIDTH - 1:CONV_WIDTH, :] + bias
            for d in range(1, CONV_WIDTH):
                acc = acc + pltpu.roll(xe, d, 0)[SUBLANES:, :] * w[CONV_WIDTH - 1 - d:CONV_WIDTH - d, :]
            return _silu(acc)

        q_pre = q_ref[:, sl].astype(F32)
        k_pre = k_ref[:, sl].astype(F32)
        q = conv_silu(q_pre, tail_ref[:, sl], cw_ref[:, sl], cb_ref[:, sl])
        k = conv_silu(k_pre, tail_ref[:, slk], cw_ref[:, slk], cb_ref[:, slk]) * scale
        tail_ref[:, sl] = q_pre[L - SUBLANES:, :]
        tail_ref[:, slk] = k_pre[L - SUBLANES:, :]
        v = v_ref[:, sl]
        qb = q.astype(BF16)
        kb = k.astype(BF16)

        m_prev = m_ref[h:h + 1, 0:1]
        r_row = grow[h:h + 1, :] - b_row[M_HEADS + h:M_HEADS + h + 1, :]
        bh_col = b_col[:, M_HEADS + h:M_HEADS + h + 1]
        r_col = gcol[:, h:h + 1] - bh_col
        big_m = jnp.maximum(jnp.max(jnp.where(causal, r_row, -jnp.inf), axis=-1, keepdims=True),
                            m_prev)
        decay_w = jnp.exp(jnp.where(causal, r_row - big_m, -jnp.inf))
        s_inter = jnp.exp(m_prev - big_m)
        floor = jnp.exp(-(bh_col + big_m))

        scores = lax.dot_general(qb, kb, (((1,), (1,)), ((), ())), preferred_element_type=F32)
        w = decay_w * scores
        c_b = c_ref[h].astype(BF16)
        num = (jnp.dot(w.astype(BF16), v, preferred_element_type=F32)
               + s_inter * jnp.dot(qb, c_b, preferred_element_type=F32))
        den = (jnp.sum(w, axis=-1, keepdims=True)
               + s_inter * jnp.sum(q * n_ref[h], axis=-1, keepdims=True))
        hh = num / jnp.maximum(jnp.abs(den), floor)

        m_end = big_m[L - 1:L, :]
        decay = jnp.exp(m_prev - m_end)
        gk = jnp.exp(r_col - m_end) * k
        c_ref[h] = decay * c_ref[h] + lax.dot_general(
            gk.astype(BF16), v, (((0,), (0,)), ((), ())), preferred_element_type=F32)
        n_ref[h] = decay * n_ref[h] + jnp.sum(gk, axis=0, keepdims=True)
        m_ref[h:h + 1, :] = jnp.broadcast_to(bh_col[L - 1:L, :] + m_end, (1, LANES))

        mu = jnp.mean(hh, axis=-1, keepdims=True)
        dlt = hh - mu
        var = jnp.mean(dlt * dlt, axis=-1, keepdims=True)
        hn = dlt * lax.rsqrt(var + HEAD_NORM_EPS) * ng_ref[:, sl]
        o_ref[:, sl] = (hn * _sigmoid(og_ref[:, sl].astype(F32))).astype(o_ref.dtype)


def _mlstm(ub3, gcol3, grow3, conv_w, conv_b, norm_g):
    bsz, s, _ = ub3.shape
    L = min(M_CHUNK, s)
    blk = lambda j: pl.BlockSpec((None, L, M_WIDTH), lambda b, c: (b, c, j))
    full = lambda a: pl.BlockSpec(a.shape, lambda b, c: (0,) * a.ndim)
    return pl.pallas_call(
        _mlstm_kernel,
        grid=(bsz, s // L),
        in_specs=[blk(0), blk(1), blk(2), blk(3),
                  pl.BlockSpec((None, L, LANES), lambda b, c: (b, c, H_WIDTH // LANES)),
                  pl.BlockSpec((None, SUBLANES, L), lambda b, c: (b, 0, c)),
                  full(conv_w), full(conv_b), full(norm_g)],
        out_specs=pl.BlockSpec((None, L, M_WIDTH), lambda b, c: (b, c, 0)),
        out_shape=jax.ShapeDtypeStruct((bsz, s, M_WIDTH), BF16),
        scratch_shapes=[pltpu.VMEM((M_HEADS, M_HEAD_DIM, M_HEAD_DIM), F32),
                        pltpu.VMEM((M_HEADS, 1, M_HEAD_DIM), F32),
                        pltpu.VMEM((SUBLANES, LANES), F32),
                        pltpu.VMEM((SUBLANES, 2 * M_WIDTH), F32)],
        compiler_params=_params("parallel", "arbitrary"),
        name="mlstm",
    )(ub3, ub3, ub3, ub3, gcol3, grow3, conv_w, conv_b, norm_g)


def _hgrn_kernel(layer, hq_ref, hf_ref, hi_ref, hg_ref, lbl_ref, ng_ref, o_ref, st_ref):
    L = hq_ref.shape[0]
    nb = L // SUBLANES

    @pl.when(pl.program_id(1) == 0)
    def _():
        st_ref[...] = jnp.zeros_like(st_ref)

    logits = lbl_ref[...]
    p = jnp.exp(logits - jnp.max(logits, axis=0, keepdims=True))
    p = p / jnp.sum(p, axis=0, keepdims=True)
    lb = jnp.zeros((1, H_WIDTH), F32)
    for d in range(1, layer + 1):
        lb = lb + p[d:d + 1, :]
    hf = hf_ref[...]
    softplus_tail = jnp.log1p(jnp.exp(-jnp.abs(hf)))
    log1m_lb = jnp.log1p(-lb)
    x1 = jnp.log(lb)
    x2 = log1m_lb + jnp.minimum(hf, 0.0) - softplus_tail
    lf = jnp.maximum(x1, x2) + jnp.log1p(jnp.exp(-jnp.abs(x1 - x2)))
    kk_all = jnp.exp(log1m_lb + jnp.minimum(-hf, 0.0) - softplus_tail)

    row = lax.broadcasted_iota(jnp.int32, (L, L), 0)
    col = lax.broadcasted_iota(jnp.int32, (L, L), 1)
    tri = (col <= row).astype(BF16)
    lf_hi, lf_lo = _split_bf16(lf)
    a_all = (jnp.dot(tri, lf_hi, preferred_element_type=F32)
             + jnp.dot(tri, lf_lo, preferred_element_type=F32))
    qq_all = _silu(hq_ref[...].astype(F32))
    row_in_blk = lax.broadcasted_iota(jnp.int32, (L, 1), 0) % SUBLANES
    diff = row - col

    for g in range(H_HEADS):
        sl = slice(g * H_DIM, (g + 1) * H_DIM)
        a = a_all[:, sl]
        kk = kk_all[:, sl]
        qq = qq_all[:, sl]
        vb = hi_ref[:, sl]

        a3 = a.reshape(nb, SUBLANES, H_DIM)
        ends = a3[:, SUBLANES - 1:SUBLANES, :]
        refs = jnp.concatenate([jnp.zeros((1, 1, H_DIM), F32), ends[:nb - 1]], axis=0)
        q_t = (qq * jnp.exp(a - jnp.broadcast_to(refs, (nb, SUBLANES, H_DIM)).reshape(L, H_DIM))
               ).astype(BF16)

        blocks = [jnp.zeros((SUBLANES, L), F32)]
        for i in range(1, nb):
            r_i = a[SUBLANES * i - 1:SUBLANES * i, :]
            k_s = (kk * jnp.exp(jnp.minimum(r_i - a, 0.0))).astype(BF16)
            blocks.append(lax.dot_general(q_t[SUBLANES * i:SUBLANES * (i + 1)], k_s,
                                          (((1,), (1,)), ((), ())), preferred_element_type=F32))
        att = jnp.where(col // SUBLANES < row // SUBLANES, jnp.concatenate(blocks, axis=0), 0.0)

        att = att + jnp.where(diff == 0, jnp.sum(qq * kk, axis=-1, keepdims=True), 0.0)
        for d in range(1, SUBLANES):
            e = jnp.exp(jnp.minimum(a - pltpu.roll(a, d, 0), 0.0))
            sc = jnp.sum(qq * e * pltpu.roll(kk, d, 0), axis=-1, keepdims=True)
            att = att + jnp.where((diff == d) & (row_in_blk >= d), sc, 0.0)

        st = st_ref[g]
        out = (jnp.dot(att.astype(BF16), vb, preferred_element_type=F32)
               + lax.dot_general((qq * jnp.exp(a)).astype(BF16), st.astype(BF16),
                                 (((1,), (1,)), ((), ())), preferred_element_type=F32))
        a_end = a[L - 1:L, :]
        st_ref[g] = jnp.exp(a_end) * st + lax.dot_general(
            vb, (kk * jnp.exp(a_end - a)).astype(BF16), (((0,), (0,)), ((), ())),
            preferred_element_type=F32)

        hn = out * lax.rsqrt(jnp.mean(out * out, axis=-1, keepdims=True) + HEAD_NORM_EPS)
        o_ref[:, sl] = (hn * ng_ref[:, sl] * _sigmoid(hg_ref[:, sl].astype(F32))).astype(o_ref.dtype)


def _hgrn(layer, ub3, uf3, lb_logits, norm_g):
    bsz, s, _ = ub3.shape
    L = min(H_CHUNK, s)
    blk = lambda j: pl.BlockSpec((None, L, H_WIDTH), lambda b, c: (b, c, j))
    full = lambda a: pl.BlockSpec(a.shape, lambda b, c: (0,) * a.ndim)
    return pl.pallas_call(
        functools.partial(_hgrn_kernel, layer),
        grid=(bsz, s // L),
        in_specs=[blk(UB_HQ), blk(0), blk(UB_HI), blk(UB_HG), full(lb_logits), full(norm_g)],
        out_specs=pl.BlockSpec((None, L, H_WIDTH), lambda b, c: (b, c, 0)),
        out_shape=jax.ShapeDtypeStruct((bsz, s, H_WIDTH), BF16),
        scratch_shapes=[pltpu.VMEM((H_HEADS, H_DIM, H_DIM), F32)],
        compiler_params=_params("parallel", "arbitrary"),
        name="hgrn2",
    )(ub3, uf3, ub3, ub3, lb_logits, norm_g)


def _merge_kernel(alpha, ha_ref, hb_ref, ga_ref, gb_ref, x_ref, pa_ref, pb_ref, wo_ref, g_ref, b_ref,
                  o_ref, ob_ref):
    ya = jnp.dot(ha_ref[...], pa_ref[...], preferred_element_type=F32)
    yb = jnp.dot(hb_ref[...], pb_ref[...], preferred_element_type=F32)
    mixed = _sigmoid(ga_ref[...].astype(F32)) * ya + _sigmoid(gb_ref[...].astype(F32)) * yb
    y = alpha * x_ref[...] + jnp.dot(mixed.astype(BF16), wo_ref[...], preferred_element_type=F32)
    out = _layernorm_rows(y, g_ref[...], b_ref[...])
    o_ref[...] = out
    ob_ref[...] = out.astype(BF16)


def _merge(alpha, ha, hb, ub, x, pa, pb, wo, g, b):
    t, d = x.shape
    tm = min(MERGE_TM, t)
    rows = lambda w, j=0: pl.BlockSpec((tm, w), lambda i: (i, j))
    full = lambda a: pl.BlockSpec(a.shape, lambda i: (0,) * a.ndim)
    return pl.pallas_call(
        functools.partial(_merge_kernel, alpha),
        grid=(t // tm,),
        in_specs=[rows(M_WIDTH), rows(H_WIDTH), rows(d, UB_GA), rows(d, UB_GB), rows(d),
                  full(pa), full(pb), full(wo), full(g), full(b)],
        out_specs=[rows(d), rows(d)],
        out_shape=[jax.ShapeDtypeStruct((t, d), F32), jax.ShapeDtypeStruct((t, d), BF16)],
        compiler_params=_params("parallel"),
        name="merge_ln",
    )(ha, hb, ub, ub, x, pa, pb, wo, g, b)


def _ffn_kernel(alpha, x_ref, xb_ref, wg_ref, wu_ref, wd_ref, g_ref, b_ref, o_ref, ob_ref):
    xb = xb_ref[...]
    gate = jnp.dot(xb, wg_ref[...], preferred_element_type=F32)
    up = jnp.dot(xb, wu_ref[...], preferred_element_type=F32)
    hid = (_silu(gate) * up).astype(BF16)
    y = alpha * x_ref[...] + jnp.dot(hid, wd_ref[...], preferred_element_type=F32)
    out = _layernorm_rows(y, g_ref[...], b_ref[...])
    o_ref[...] = out
    ob_ref[...] = out.astype(BF16)


def _ffn(alpha, x, xb, wg, wu, wd, g, b):
    t, d = x.shape
    tm = min(FFN_TM, t)
    rows = pl.BlockSpec((tm, d), lambda i: (i, 0))
    full = lambda a: pl.BlockSpec(a.shape, lambda i: (0,) * a.ndim, pipeline_mode=pl.Buffered(1))
    return pl.pallas_call(
        functools.partial(_ffn_kernel, alpha),
        grid=(t // tm,),
        in_specs=[rows, rows, full(wg), full(wu), full(wd), full(g), full(b)],
        out_specs=[rows, rows],
        out_shape=[jax.ShapeDtypeStruct((t, d), F32), jax.ShapeDtypeStruct((t, d), BF16)],
        compiler_params=_params("parallel"),
        name="ffn_ln",
    )(x, xb, wg, wu, wd, g, b)


def _in_proj_layout(w_in, b_in):
    sizes = (M_WIDTH,) * 4 + (M_HEADS, M_HEADS) + (H_WIDTH,) * 6
    off = [int(o) for o in np.cumsum((0,) + sizes)]
    seg = lambda a, i: a[..., off[i]:off[i + 1]]
    order_b = (0, 1, 2, 3, 6, 8, 9, 10, 11)
    wb = jnp.concatenate([seg(w_in, i) for i in order_b], axis=-1).astype(BF16)
    bb = jnp.concatenate([seg(b_in, i) for i in order_b], axis=-1)[:, None, :]
    pad = LANES - 2 * M_HEADS
    wf = jnp.concatenate([seg(w_in, 7), seg(w_in, 4), seg(w_in, 5),
                          jnp.zeros(w_in.shape[:2] + (pad,), w_in.dtype)], axis=-1).astype(BF16)
    bf = jnp.concatenate([seg(b_in, 7), seg(b_in, 4), seg(b_in, 5),
                          jnp.zeros(b_in.shape[:1] + (pad,), b_in.dtype)], axis=-1)[:, None, :]
    return wb, bb, wf, bf


def kernel(x, w_in, b_in, conv_w, conv_b, m_norm_g, lb_logits, h_norm_g, w_proj_a, w_proj_b, w_out,
           ln1_g, ln1_b, w_ffn_gate, w_ffn_up, w_ffn_down, ln2_g, ln2_b):
    bsz, s, d = x.shape
    depth = w_in.shape[0]
    alpha = float((2 * depth) ** 0.25)
    t = bsz * s

    wb, bb, wf, bf = _in_proj_layout(w_in, b_in)
    pa = w_proj_a.astype(BF16)
    pb = w_proj_b.astype(BF16)
    wo = w_out.astype(BF16)
    wg = w_ffn_gate.astype(BF16)
    wu = w_ffn_up.astype(BF16)
    wd = w_ffn_down.astype(BF16)

    xf = x.reshape(t, d)
    xb = xf.astype(BF16)
    for layer in range(depth):
        ub = _proj(xb, wb[layer], bb[layer], BF16, PROJ_TN)
        uf = _proj(xb, wf[layer], bf[layer], F32, UF_WIDTH)
        ub3 = ub.reshape(bsz, s, UB_WIDTH)
        uf3 = uf.reshape(bsz, s, UF_WIDTH)
        grow3 = jnp.swapaxes(uf3[:, :, H_WIDTH:H_WIDTH + SUBLANES], 1, 2)
        ha = _mlstm(ub3, uf3, grow3, conv_w[layer], conv_b[layer][None, :], m_norm_g[layer][None, :])
        hb = _hgrn(layer, ub3, uf3, lb_logits, h_norm_g[layer][None, :])
        xf, xb = _merge(alpha, ha.reshape(t, M_WIDTH), hb.reshape(t, H_WIDTH), ub, xf,
                        pa[layer], pb[layer], wo[layer], ln1_g[layer][None, :], ln1_b[layer][None, :])
        xf, xb = _ffn(alpha, xf, xb, wg[layer], wu[layer], wd[layer],
                      ln2_g[layer][None, :], ln2_b[layer][None, :])
    return xf.reshape(bsz, s, d)
```

```python
import functools

import numpy as np
import jax
import jax.numpy as jnp
from jax import lax
from jax.experimental import pallas as pl
from jax.experimental.pallas import tpu as pltpu

F32 = jnp.float32
BF16 = jnp.bfloat16

M_HEADS = 4
M_HEAD_DIM = 512
M_WIDTH = M_HEADS * M_HEAD_DIM
CONV_WIDTH = 4
H_HEADS = 8
H_DIM = 128
H_WIDTH = H_HEADS * H_DIM
LN_EPS = 1e-5
HEAD_NORM_EPS = 1e-6

SUBLANES = 8
LANES = 128
VMEM_LIMIT_BYTES = 56 * 1024 * 1024

MIX_CHUNK = 256
H_CHUNK = 64
MERGE_TM = 512
FFN_TM = 512

COL_Q, COL_K, COL_V, COL_O = 0, M_WIDTH, 2 * M_WIDTH, 3 * M_WIDTH
COL_HQ = 4 * M_WIDTH
COL_HI = COL_HQ + H_WIDTH
COL_HG = COL_HI + H_WIDTH
COL_GAB = COL_HG + H_WIDTH
WB_WIDTH = COL_GAB + 2 * H_WIDTH
WF_WIDTH = H_WIDTH + LANES


def _sigmoid(x):
    return 1.0 / (1.0 + jnp.exp(-x))


def _silu(x):
    return x * _sigmoid(x)


def _log_sigmoid(x):
    return jnp.minimum(x, 0.0) - jnp.log1p(jnp.exp(-jnp.abs(x)))


def _split_bf16(x):
    hi = x.astype(BF16)
    lo = (x - hi.astype(F32)).astype(BF16)
    return hi, lo


def _dot(a, b):
    return jnp.dot(a, b, preferred_element_type=F32)


def _dot_nt(a, b):
    return lax.dot_general(a, b, (((1,), (1,)), ((), ())), preferred_element_type=F32)


def _dot_tn(a, b):
    return lax.dot_general(a, b, (((0,), (0,)), ((), ())), preferred_element_type=F32)


def _layernorm_rows(y, g, b):
    mu = jnp.mean(y, axis=-1, keepdims=True)
    d = y - mu
    var = jnp.mean(d * d, axis=-1, keepdims=True)
    return d * lax.rsqrt(var + LN_EPS) * g + b


def _params(*semantics):
    return pltpu.CompilerParams(dimension_semantics=semantics, vmem_limit_bytes=VMEM_LIMIT_BYTES)


def _conv_silu_permuted(pre, tail, w, bias):
    rows = pre.shape[0]
    sub = lax.broadcasted_iota(jnp.int32, (SUBLANES, pre.shape[1]), 0)
    acc = pre * w[CONV_WIDTH - 1:CONV_WIDTH, :] + bias
    for d in range(1, CONV_WIDTH):
        cur = pre[rows - d * SUBLANES:, :]
        prv = tail[(CONV_WIDTH - 1 - d) * SUBLANES:, :]
        tiles = []
        for i in range(d):
            ts = slice(i * SUBLANES, (i + 1) * SUBLANES)
            tiles.append(jnp.where(sub == 0, pltpu.roll(prv[ts], 1, 0), pltpu.roll(cur[ts], 1, 0)))
        shifted = jnp.concatenate(tiles + [pre[:rows - d * SUBLANES, :]], axis=0)
        acc = acc + shifted * w[CONV_WIDTH - 1 - d:CONV_WIDTH - d, :]
    return _silu(acc)


def _mlstm_chunk(xp, causal, wb_ref, bb_ref, gcol, grow, b_col, b_row, cw_ref, cb_ref, ng_ref,
                 c_ref, n_ref, m_ref, tail_ref, after_head):
    L = xp.shape[0]
    dh = M_HEAD_DIM
    log_scale = float(np.log(dh ** -0.5))
    tail_rows = (CONV_WIDTH - 1) * SUBLANES
    outs = []
    for h in range(M_HEADS):
        sl = slice(h * dh, (h + 1) * dh)
        slk = slice(M_WIDTH + h * dh, M_WIDTH + (h + 1) * dh)
        proj = lambda col: _dot(xp, wb_ref[:, col + h * dh:col + (h + 1) * dh]) \
            + bb_ref[:, col + h * dh:col + (h + 1) * dh]

        q_pre = proj(COL_Q)
        k_pre = proj(COL_K)
        q = _conv_silu_permuted(q_pre, tail_ref[:, sl], cw_ref[:, sl], cb_ref[:, sl])
        k = _conv_silu_permuted(k_pre, tail_ref[:, slk], cw_ref[:, slk], cb_ref[:, slk])
        tail_ref[:, sl] = q_pre[L - tail_rows:, :]
        tail_ref[:, slk] = k_pre[L - tail_rows:, :]
        v = proj(COL_V).astype(BF16)
        qb = q.astype(BF16)
        kb = k.astype(BF16)

        m_prev = m_ref[h:h + 1, 0:1]
        r_row = grow[h:h + 1, :] - b_row[M_HEADS + h:M_HEADS + h + 1, :]
        bh_col = b_col[:, M_HEADS + h:M_HEADS + h + 1]
        r_col = gcol[:, h:h + 1] - bh_col
        big_m = jnp.maximum(jnp.max(jnp.where(causal, r_row, -jnp.inf), axis=-1, keepdims=True),
                            m_prev)
        decay_w = jnp.exp(jnp.where(causal, (r_row + log_scale) - big_m, -jnp.inf))
        s_inter = jnp.exp(m_prev - big_m)
        floor = jnp.exp(-(bh_col + big_m))

        w = decay_w * _dot_nt(qb, kb)
        num = _dot(w.astype(BF16), v) + s_inter * _dot(qb, c_ref[h].astype(BF16))
        den = (jnp.sum(w, axis=-1, keepdims=True)
               + s_inter * jnp.sum(q * n_ref[h], axis=-1, keepdims=True))
        hh = num / jnp.maximum(jnp.abs(den), floor)

        m_end = big_m[L - 1:L, :]
        decay = jnp.exp(m_prev - m_end)
        gk = jnp.exp((r_col + log_scale) - m_end) * k
        c_ref[h] = decay * c_ref[h] + _dot_tn(gk.astype(BF16), v)
        n_ref[h] = decay * n_ref[h] + jnp.sum(gk, axis=0, keepdims=True)
        m_ref[h:h + 1, :] = jnp.broadcast_to(bh_col[L - 1:L, :] + m_end, (1, LANES))

        mu = jnp.mean(hh, axis=-1, keepdims=True)
        dlt = hh - mu
        var = jnp.mean(dlt * dlt, axis=-1, keepdims=True)
        hn = dlt * lax.rsqrt(var + HEAD_NORM_EPS) * ng_ref[:, sl]
        outs.append((hn * _sigmoid(proj(COL_O))).astype(BF16))
        after_head(h)
    return jnp.concatenate(outs, axis=-1)


def _block_ref_rows(a, h):
    L, n = a.shape
    if h >= SUBLANES:
        a4 = a.reshape(L // (2 * h), 2 * h, n)
        return jnp.broadcast_to(a4[:, h - 1:h, :], a4.shape).reshape(L, n)
    a3 = a.reshape(L // SUBLANES, SUBLANES, n)
    sub = lax.broadcasted_iota(jnp.int32, a3.shape, 1)
    if h == 4:
        ref = jnp.broadcast_to(a3[:, 3:4, :], a3.shape)
    elif h == 2:
        ref = jnp.where(sub < 4, jnp.broadcast_to(a3[:, 1:2, :], a3.shape),
                        jnp.broadcast_to(a3[:, 5:6, :], a3.shape))
    else:
        ref = jnp.where(sub % 2 == 1, pltpu.roll(a, 1, 0).reshape(a3.shape), a3)
    return ref.reshape(L, n)


def _hgrn_chunk(layer, xc, wb_ref, bb_ref, wf_ref, bf_ref, lbl_ref, ng_ref, st_ref, hb_ref):
    L = xc.shape[0]
    S = min(H_CHUNK, L)
    levels = [1 << i for i in range(S.bit_length() - 1)]
    proj = lambda col: _dot(xc, wb_ref[:, col:col + H_WIDTH]) + bb_ref[:, col:col + H_WIDTH]

    logits = lbl_ref[...]
    p = jnp.exp(logits - jnp.max(logits, axis=0, keepdims=True))
    p = p / jnp.sum(p, axis=0, keepdims=True)
    lb = jnp.zeros((1, H_WIDTH), F32)
    for d in range(1, layer + 1):
        lb = lb + p[d:d + 1, :]
    hf = _dot(xc, wf_ref[:, :H_WIDTH]) + bf_ref[:, :H_WIDTH]
    softplus_tail = jnp.log1p(jnp.exp(-jnp.abs(hf)))
    log1m_lb = jnp.log1p(-lb)
    x1 = jnp.log(lb)
    x2 = log1m_lb + jnp.minimum(hf, 0.0) - softplus_tail
    lf = jnp.maximum(x1, x2) + jnp.log1p(jnp.exp(-jnp.abs(x1 - x2)))
    kk_all = jnp.exp(log1m_lb + jnp.minimum(-hf, 0.0) - softplus_tail)
    qq_all = _silu(proj(COL_HQ))
    v_all = proj(COL_HI).astype(BF16)
    og_all = _sigmoid(proj(COL_HG)) * ng_ref[...]

    rl = lax.broadcasted_iota(jnp.int32, (L, L), 0)
    cl = lax.broadcasted_iota(jnp.int32, (L, L), 1)
    tri = ((cl <= rl) & (rl // S == cl // S)).astype(BF16)
    lf_hi, lf_lo = _split_bf16(lf)
    a_all = _dot(tri, lf_hi) + _dot(tri, lf_lo)

    row = lax.broadcasted_iota(jnp.int32, (S, S), 0)
    col = lax.broadcasted_iota(jnp.int32, (S, S), 1)
    trow = lax.broadcasted_iota(jnp.int32, (S, 1), 0)
    upper = [(trow // h) % 2 == 1 for h in levels]
    pair = [(row // (2 * h) == col // (2 * h)) & ((row // h) % 2 == 1) & ((col // h) % 2 == 0)
            for h in levels]

    def sub_chunk(sc):
        rs = slice(sc * S, (sc + 1) * S)
        for g in range(H_HEADS):
            sl = slice(g * H_DIM, (g + 1) * H_DIM)
            a = a_all[rs, sl]
            kk = kk_all[rs, sl]
            qq = qq_all[rs, sl]
            vb = v_all[rs, sl]

            att = jnp.where(row == col, jnp.sum(qq * kk, axis=-1, keepdims=True), 0.0)
            for h, up, pr in zip(levels, upper, pair):
                fac = jnp.exp(-jnp.abs(a - _block_ref_rows(a, h)))
                pm = (jnp.where(up, qq, kk) * fac).astype(BF16)
                att = jnp.where(pr, _dot_nt(pm, pm), att)

            st = st_ref[g]
            out = _dot(att.astype(BF16), vb) + _dot_nt((qq * jnp.exp(a)).astype(BF16), st.astype(BF16))
            a_end = a[S - 1:S, :]
            st_ref[g] = jnp.exp(a_end) * st + _dot_tn(vb, (kk * jnp.exp(a_end - a)).astype(BF16))

            hn = out * lax.rsqrt(jnp.mean(out * out, axis=-1, keepdims=True) + HEAD_NORM_EPS)
            hb_ref[rs, sl] = (hn * og_all[rs, sl]).astype(hb_ref.dtype)

    return sub_chunk, L // S


def _mixer_kernel(layer, x_ref, wb_ref, bb_ref, wf_ref, bf_ref, wgt_ref, bgt_ref, cw_ref, cb_ref,
                  mng_ref, lbl_ref, hng_ref, ha_ref, hb_ref, gab_ref,
                  c_ref, n_ref, m_ref, tail_ref, st_ref):
    L = x_ref.shape[0]
    nv = L // SUBLANES

    @pl.when(pl.program_id(1) == 0)
    def _():
        c_ref[...] = jnp.zeros_like(c_ref)
        n_ref[...] = jnp.zeros_like(n_ref)
        m_ref[...] = jnp.zeros_like(m_ref)
        tail_ref[...] = jnp.zeros_like(tail_ref)
        st_ref[...] = jnp.zeros_like(st_ref)

    xc = x_ref[...]

    hgrn_sub_chunk, n_sub = _hgrn_chunk(layer, xc, wb_ref, bb_ref, wf_ref, bf_ref, lbl_ref, hng_ref,
                                        st_ref, hb_ref)
    sub_per_head = -(-n_sub // M_HEADS)

    def after_head(h):
        for sc in range(h * sub_per_head, min((h + 1) * sub_per_head, n_sub)):
            hgrn_sub_chunk(sc)

    rr = lax.broadcasted_iota(jnp.int32, (L, L), 0)
    cc = lax.broadcasted_iota(jnp.int32, (L, L), 1)
    t_r = (rr % SUBLANES) * nv + rr // SUBLANES
    t_c = (cc % SUBLANES) * nv + cc // SUBLANES
    xp = _dot((cc == t_r).astype(BF16), xc).astype(BF16)
    causal = t_c <= t_r
    tri = causal.astype(BF16)
    tri_t = (t_r <= t_c).astype(BF16)

    gcol = _dot(xp, wf_ref[:, H_WIDTH:]) + bf_ref[:, H_WIDTH:]
    grow = _dot_nt(wgt_ref[...], xp) + bgt_ref[...]
    lf_hi, lf_lo = _split_bf16(_log_sigmoid(gcol))
    b_col = _dot(tri, lf_hi) + _dot(tri, lf_lo)
    lr_hi, lr_lo = _split_bf16(_log_sigmoid(grow))
    b_row = _dot(lr_hi, tri_t) + _dot(lr_lo, tri_t)

    h_perm = _mlstm_chunk(xp, causal, wb_ref, bb_ref, gcol, grow, b_col, b_row, cw_ref, cb_ref, mng_ref,
                          c_ref, n_ref, m_ref, tail_ref, after_head)
    ha_ref[...] = _dot((rr == t_c).astype(BF16), h_perm).astype(ha_ref.dtype)

    gab_ref[...] = (_dot(xc, wb_ref[:, COL_GAB:]) + bb_ref[:, COL_GAB:]).astype(gab_ref.dtype)


def _mixer(layer, xb3, wb, bb, wf, bf, wgt, bgt, conv_w, conv_b, m_norm_g, lb_logits, h_norm_g):
    bsz, s, d = xb3.shape
    L = min(MIX_CHUNK, s)
    resident = lambda a: pl.BlockSpec(a.shape, lambda b, c: (0,) * a.ndim, pipeline_mode=pl.Buffered(1))
    rows = lambda w: pl.BlockSpec((None, L, w), lambda b, c: (b, c, 0))
    consts = (wb, bb, wf, bf, wgt, bgt, conv_w, conv_b, m_norm_g, lb_logits, h_norm_g)
    return pl.pallas_call(
        functools.partial(_mixer_kernel, layer),
        grid=(bsz, s // L),
        in_specs=[rows(d)] + [resident(a) for a in consts],
        out_specs=[rows(M_WIDTH), rows(H_WIDTH), rows(2 * H_WIDTH)],
        out_shape=[jax.ShapeDtypeStruct((bsz, s, M_WIDTH), BF16),
                   jax.ShapeDtypeStruct((bsz, s, H_WIDTH), BF16),
                   jax.ShapeDtypeStruct((bsz, s, 2 * H_WIDTH), BF16)],
        scratch_shapes=[pltpu.VMEM((M_HEADS, M_HEAD_DIM, M_HEAD_DIM), F32),
                        pltpu.VMEM((M_HEADS, 1, M_HEAD_DIM), F32),
                        pltpu.VMEM((SUBLANES, LANES), F32),
                        pltpu.VMEM(((CONV_WIDTH - 1) * SUBLANES, 2 * M_WIDTH), F32),
                        pltpu.VMEM((H_HEADS, H_DIM, H_DIM), F32)],
        compiler_params=_params("parallel", "arbitrary"),
        name="mixer",
    )(xb3, *consts)


def _merge_kernel(alpha, ha_ref, hb_ref, ga_ref, gb_ref, x_ref, pa_ref, pb_ref, wo_ref, g_ref, b_ref,
                  o_ref, ob_ref):
    ya = _dot(ha_ref[...], pa_ref[...])
    yb = _dot(hb_ref[...], pb_ref[...])
    mixed = _sigmoid(ga_ref[...].astype(F32)) * ya + _sigmoid(gb_ref[...].astype(F32)) * yb
    y = alpha * x_ref[...] + _dot(mixed.astype(BF16), wo_ref[...])
    out = _layernorm_rows(y, g_ref[...], b_ref[...])
    o_ref[...] = out
    ob_ref[...] = out.astype(BF16)


def _merge(alpha, ha, hb, gab, x, pa, pb, wo, g, b):
    t, d = x.shape
    tm = min(MERGE_TM, t)
    rows = lambda w, j=0: pl.BlockSpec((tm, w), lambda i: (i, j))
    full = lambda a: pl.BlockSpec(a.shape, lambda i: (0,) * a.ndim)
    return pl.pallas_call(
        functools.partial(_merge_kernel, alpha),
        grid=(t // tm,),
        in_specs=[rows(M_WIDTH), rows(H_WIDTH), rows(d, 0), rows(d, 1), rows(d),
                  full(pa), full(pb), full(wo), full(g), full(b)],
        out_specs=[rows(d), rows(d)],
        out_shape=[jax.ShapeDtypeStruct((t, d), F32), jax.ShapeDtypeStruct((t, d), BF16)],
        compiler_params=_params("parallel"),
        name="merge_ln",
    )(ha, hb, gab, gab, x, pa, pb, wo, g, b)


def _ffn_kernel(alpha, x_ref, xb_ref, wg_ref, wu_ref, wd_ref, g_ref, b_ref, o_ref, ob_ref):
    xb = xb_ref[...]
    gate = _dot(xb, wg_ref[...])
    up = _dot(xb, wu_ref[...])
    hid = (_silu(gate) * up).astype(BF16)
    y = alpha * x_ref[...] + _dot(hid, wd_ref[...])
    out = _layernorm_rows(y, g_ref[...], b_ref[...])
    o_ref[...] = out
    ob_ref[...] = out.astype(BF16)


def _ffn(alpha, x, xb, wg, wu, wd, g, b):
    t, d = x.shape
    tm = min(FFN_TM, t)
    rows = pl.BlockSpec((tm, d), lambda i: (i, 0))
    full = lambda a: pl.BlockSpec(a.shape, lambda i: (0,) * a.ndim, pipeline_mode=pl.Buffered(1))
    return pl.pallas_call(
        functools.partial(_ffn_kernel, alpha),
        grid=(t // tm,),
        in_specs=[rows, rows, full(wg), full(wu), full(wd), full(g), full(b)],
        out_specs=[rows, rows],
        out_shape=[jax.ShapeDtypeStruct((t, d), F32), jax.ShapeDtypeStruct((t, d), BF16)],
        compiler_params=_params("parallel"),
        name="ffn_ln",
    )(x, xb, wg, wu, wd, g, b)


def _in_proj_layout(w_in, b_in):
    sizes = (M_WIDTH,) * 4 + (M_HEADS, M_HEADS) + (H_WIDTH,) * 6
    off = [int(o) for o in np.cumsum((0,) + sizes)]
    seg = lambda a, i: a[..., off[i]:off[i + 1]]
    order_b = (0, 1, 2, 3, 6, 8, 9, 10, 11)
    wb = jnp.concatenate([seg(w_in, i) for i in order_b], axis=-1).astype(BF16)
    bb = jnp.concatenate([seg(b_in, i) for i in order_b], axis=-1)[:, None, :]
    pad = LANES - 2 * M_HEADS
    wf = jnp.concatenate([seg(w_in, 7), seg(w_in, 4), seg(w_in, 5),
                          jnp.zeros(w_in.shape[:2] + (pad,), w_in.dtype)], axis=-1).astype(BF16)
    bf = jnp.concatenate([seg(b_in, 7), seg(b_in, 4), seg(b_in, 5),
                          jnp.zeros(b_in.shape[:1] + (pad,), b_in.dtype)], axis=-1)[:, None, :]
    wgt = jnp.swapaxes(jnp.concatenate([seg(w_in, 4), seg(w_in, 5)], axis=-1), 1, 2).astype(BF16)
    bgt = jnp.concatenate([seg(b_in, 4), seg(b_in, 5)], axis=-1)[:, :, None]
    return wb, bb, wf, bf, wgt, bgt


def kernel(x, w_in, b_in, conv_w, conv_b, m_norm_g, lb_logits, h_norm_g, w_proj_a, w_proj_b, w_out,
           ln1_g, ln1_b, w_ffn_gate, w_ffn_up, w_ffn_down, ln2_g, ln2_b):
    bsz, s, d = x.shape
    depth = w_in.shape[0]
    alpha = float((2 * depth) ** 0.25)
    t = bsz * s

    wb, bb, wf, bf, wgt, bgt = _in_proj_layout(w_in, b_in)
    pa = w_proj_a.astype(BF16)
    pb = w_proj_b.astype(BF16)
    wo = w_out.astype(BF16)
    wg = w_ffn_gate.astype(BF16)
    wu = w_ffn_up.astype(BF16)
    wd = w_ffn_down.astype(BF16)

    xf = x.reshape(t, d)
    xb = xf.astype(BF16)
    for layer in range(depth):
        ha, hb, gab = _mixer(layer, xb.reshape(bsz, s, d), wb[layer], bb[layer], wf[layer], bf[layer],
                             wgt[layer], bgt[layer], conv_w[layer], conv_b[layer][None, :],
                             m_norm_g[layer][None, :], lb_logits, h_norm_g[layer][None, :])
        xf, xb = _merge(alpha, ha.reshape(t, M_WIDTH), hb.reshape(t, H_WIDTH),
                        gab.reshape(t, 2 * H_WIDTH), xf,
                        pa[layer], pb[layer], wo[layer], ln1_g[layer][None, :], ln1_b[layer][None, :])
        xf, xb = _ffn(alpha, xf, xb, wg[layer], wu[layer], wd[layer],
                      ln2_g[layer][None, :], ln2_b[layer][None, :])
    return xf.reshape(bsz, s, d)
```

```python
import functools

import numpy as np
import jax
import jax.numpy as jnp
from jax import lax
from jax.experimental import pallas as pl
from jax.experimental.pallas import tpu as pltpu

F32 = jnp.float32
BF16 = jnp.bfloat16

M_HEADS = 4
M_HEAD_DIM = 512
M_WIDTH = M_HEADS * M_HEAD_DIM
CONV_WIDTH = 4
H_HEADS = 8
H_DIM = 128
H_WIDTH = H_HEADS * H_DIM
LN_EPS = 1e-5
HEAD_NORM_EPS = 1e-6

SUBLANES = 8
LANES = 128
VMEM_LIMIT_BYTES = 60 * 1024 * 1024

MIX_CHUNK = 256
H_CHUNK = 128
MERGE_TM = 512
FFN_TM = 512

HEAD_COLS = 4 * M_HEAD_DIM
COL_HQ = M_HEADS * HEAD_COLS
PIECE_COLS = 256


def _sigmoid(x):
    return 1.0 / (1.0 + jnp.exp(-x))


def _silu(x):
    return x * _sigmoid(x)


def _log_sigmoid(x):
    return jnp.minimum(x, 0.0) - jnp.log1p(jnp.exp(-jnp.abs(x)))


def _split_bf16(x):
    hi = x.astype(BF16)
    lo = (x - hi.astype(F32)).astype(BF16)
    return hi, lo


def _dot(a, b):
    return jnp.dot(a, b, preferred_element_type=F32)


def _dot_nt(a, b):
    return lax.dot_general(a, b, (((1,), (1,)), ((), ())), preferred_element_type=F32)


def _dot_tn(a, b):
    return lax.dot_general(a, b, (((0,), (0,)), ((), ())), preferred_element_type=F32)


def _layernorm_rows(y, g, b):
    mu = jnp.mean(y, axis=-1, keepdims=True)
    d = y - mu
    var = jnp.mean(d * d, axis=-1, keepdims=True)
    return d * lax.rsqrt(var + LN_EPS) * g + b


def _params(*semantics):
    return pltpu.CompilerParams(dimension_semantics=semantics, vmem_limit_bytes=VMEM_LIMIT_BYTES)


def _conv_silu_permuted(pre, tail, w, bias):
    rows = pre.shape[0]
    sub = lax.broadcasted_iota(jnp.int32, (SUBLANES, pre.shape[1]), 0)
    acc = pre * w[CONV_WIDTH - 1:CONV_WIDTH, :] + bias
    for d in range(1, CONV_WIDTH):
        cur = pre[rows - d * SUBLANES:, :]
        prv = tail[(CONV_WIDTH - 1 - d) * SUBLANES:, :]
        tiles = []
        for i in range(d):
            ts = slice(i * SUBLANES, (i + 1) * SUBLANES)
            tiles.append(jnp.where(sub == 0, pltpu.roll(prv[ts], 1, 0), pltpu.roll(cur[ts], 1, 0)))
        shifted = jnp.concatenate(tiles + [pre[:rows - d * SUBLANES, :]], axis=0)
        acc = acc + shifted * w[CONV_WIDTH - 1 - d:CONV_WIDTH - d, :]
    return _silu(acc)


def _mlstm_chunk(ub_ref, causal, gcol, grow, b_col, b_row, cw_ref, cb_ref, ng_ref,
                 c_ref, n_ref, m_ref, tail_ref, head_filler, after_head):
    L = ub_ref.shape[0]
    dh = M_HEAD_DIM
    log_scale = float(np.log(dh ** -0.5))
    tail_rows = (CONV_WIDTH - 1) * SUBLANES
    outs = []
    for h in range(M_HEADS):
        sl = slice(h * dh, (h + 1) * dh)
        slk = slice(M_WIDTH + h * dh, M_WIDTH + (h + 1) * dh)
        c0 = h * HEAD_COLS
        q_pre = ub_ref[:, c0:c0 + dh].astype(F32)
        k_pre = ub_ref[:, c0 + dh:c0 + 2 * dh].astype(F32)
        v = ub_ref[:, c0 + 2 * dh:c0 + 3 * dh]
        o_gate = ub_ref[:, c0 + 3 * dh:c0 + 4 * dh]
        fill = head_filler(h)

        fill()
        q = _conv_silu_permuted(q_pre, tail_ref[:, sl], cw_ref[:, sl], cb_ref[:, sl])
        fill()
        k = _conv_silu_permuted(k_pre, tail_ref[:, slk], cw_ref[:, slk], cb_ref[:, slk])
        tail_ref[:, sl] = q_pre[L - tail_rows:, :]
        tail_ref[:, slk] = k_pre[L - tail_rows:, :]
        qb = q.astype(BF16)
        kb = k.astype(BF16)
        fill()

        m_prev = m_ref[h:h + 1, 0:1]
        r_row = grow[h:h + 1, :] - b_row[M_HEADS + h:M_HEADS + h + 1, :]
        bh_col = b_col[:, M_HEADS + h:M_HEADS + h + 1]
        r_col = gcol[:, h:h + 1] - bh_col
        big_m = jnp.maximum(jnp.max(jnp.where(causal, r_row, -jnp.inf), axis=-1, keepdims=True),
                            m_prev)
        decay_w = jnp.exp(jnp.where(causal, (r_row + log_scale) - big_m, -jnp.inf))
        s_inter = jnp.exp(m_prev - big_m)
        floor = jnp.exp(-(bh_col + big_m))

        w = decay_w * _dot_nt(qb, kb)
        fill()
        num = _dot(w.astype(BF16), v) + s_inter * _dot(qb, c_ref[h].astype(BF16))
        den = (jnp.sum(w, axis=-1, keepdims=True)
               + s_inter * jnp.sum(q * n_ref[h], axis=-1, keepdims=True))
        hh = num / jnp.maximum(jnp.abs(den), floor)
        fill()

        m_end = big_m[L - 1:L, :]
        decay = jnp.exp(m_prev - m_end)
        gk = jnp.exp((r_col + log_scale) - m_end) * k
        c_ref[h] = decay * c_ref[h] + _dot_tn(gk.astype(BF16), v)
        n_ref[h] = decay * n_ref[h] + jnp.sum(gk, axis=0, keepdims=True)
        m_ref[h:h + 1, :] = jnp.broadcast_to(bh_col[L - 1:L, :] + m_end, (1, LANES))
        fill()

        mu = jnp.mean(hh, axis=-1, keepdims=True)
        dlt = hh - mu
        var = jnp.mean(dlt * dlt, axis=-1, keepdims=True)
        hn = dlt * lax.rsqrt(var + HEAD_NORM_EPS) * ng_ref[:, sl]
        outs.append((hn * _sigmoid(o_gate.astype(F32))).astype(BF16))
        after_head(h, fill)
    return jnp.concatenate(outs, axis=-1)


def _block_ref_rows(a, h):
    L, n = a.shape
    if h >= SUBLANES:
        a4 = a.reshape(L // (2 * h), 2 * h, n)
        return jnp.broadcast_to(a4[:, h - 1:h, :], a4.shape).reshape(L, n)
    a3 = a.reshape(L // SUBLANES, SUBLANES, n)
    sub = lax.broadcasted_iota(jnp.int32, a3.shape, 1)
    if h == 4:
        ref = jnp.broadcast_to(a3[:, 3:4, :], a3.shape)
    elif h == 2:
        ref = jnp.where(sub < 4, jnp.broadcast_to(a3[:, 1:2, :], a3.shape),
                        jnp.broadcast_to(a3[:, 5:6, :], a3.shape))
    else:
        ref = jnp.where(sub % 2 == 1, pltpu.roll(a, 1, 0).reshape(a3.shape), a3)
    return ref.reshape(L, n)


def _hgrn_chunk(layer, proj, lbl_ref, ng_ref, st_ref, hb_ref):
    L = proj.shape[0]
    S = min(H_CHUNK, L)
    levels = [1 << i for i in range(S.bit_length() - 1)]
    hq = proj[:, :H_WIDTH]
    v_all = proj[:, H_WIDTH:2 * H_WIDTH].astype(BF16)
    hg = proj[:, 2 * H_WIDTH:3 * H_WIDTH]
    hf = proj[:, 3 * H_WIDTH:]

    logits = lbl_ref[...]
    p = jnp.exp(logits - jnp.max(logits, axis=0, keepdims=True))
    p = p / jnp.sum(p, axis=0, keepdims=True)
    lb = jnp.zeros((1, H_WIDTH), F32)
    for d in range(1, layer + 1):
        lb = lb + p[d:d + 1, :]
    softplus_tail = jnp.log(1.0 + jnp.exp(-jnp.abs(hf)))
    log1m_lb = jnp.log1p(-lb)
    x2 = log1m_lb + jnp.minimum(hf, 0.0) - softplus_tail
    if layer == 0:
        lf = x2
    else:
        x1 = jnp.log(lb)
        lf = jnp.maximum(x1, x2) + jnp.log(1.0 + jnp.exp(-jnp.abs(x1 - x2)))
    kk_all = jnp.exp(log1m_lb - jnp.maximum(hf, 0.0) - softplus_tail)
    qq_all = _silu(hq)
    og_all = _sigmoid(hg) * ng_ref[...]

    rl = lax.broadcasted_iota(jnp.int32, (L, L), 0)
    cl = lax.broadcasted_iota(jnp.int32, (L, L), 1)
    tri = ((cl <= rl) & (rl // S == cl // S)).astype(BF16)
    lf_hi, lf_lo = _split_bf16(lf)
    a_all = _dot(tri, lf_hi) + _dot(tri, lf_lo)

    D = H_DIM
    row = lax.broadcasted_iota(jnp.int32, (S, 2 * S), 0)
    col = lax.broadcasted_iota(jnp.int32, (S, 2 * S), 1) % S
    trow = lax.broadcasted_iota(jnp.int32, (S, 1), 0)
    upper = [(trow // h) % 2 == 1 for h in levels]
    pair = [(row // (2 * h) == col // (2 * h)) & ((row // h) % 2 == 1) & ((col // h) % 2 == 0)
            for h in levels]
    first = lax.broadcasted_iota(jnp.int32, (S, 2 * S), 1) < S

    def block_diag(x):
        z = jnp.zeros((x.shape[0], D), x.dtype)
        return jnp.concatenate([jnp.concatenate([x[:, :D], z], axis=1),
                                jnp.concatenate([z, x[:, D:]], axis=1)], axis=0)

    def per_head(fn, x):
        return jnp.concatenate([fn(x[:, :D]), fn(x[:, D:])], axis=1)

    def unit(sc, gp):
        rs = slice(sc * S, (sc + 1) * S)
        sl = slice(2 * gp * D, (2 * gp + 2) * D)
        a = a_all[rs, sl]
        kk = kk_all[rs, sl]
        qq = qq_all[rs, sl]
        vb = v_all[rs, sl]

        qk = qq * kk
        diag = jnp.where(first, jnp.sum(qk[:, :D], axis=-1, keepdims=True),
                         jnp.sum(qk[:, D:], axis=-1, keepdims=True))
        att = jnp.where(row == col, diag, 0.0)
        for h, up, pr in zip(levels, upper, pair):
            fac = jnp.exp(-jnp.abs(a - _block_ref_rows(a, h)))
            pm = (jnp.where(up, qq, kk) * fac).astype(BF16)
            att = jnp.where(pr, _dot_nt(pm, block_diag(pm)), att)

        st = jnp.concatenate([st_ref[2 * gp], st_ref[2 * gp + 1]], axis=1)
        out = (_dot(att.astype(BF16), block_diag(vb))
               + _dot_nt((qq * jnp.exp(a)).astype(BF16), block_diag(st.astype(BF16))))
        a_end = a[S - 1:S, :]
        upd = _dot_tn(vb, (kk * jnp.exp(a_end - a)).astype(BF16))
        st_new = jnp.exp(a_end) * st + jnp.concatenate([upd[:D, :D], upd[D:, D:]], axis=1)
        st_ref[2 * gp] = st_new[:, :D]
        st_ref[2 * gp + 1] = st_new[:, D:]

        ms = per_head(lambda o: jnp.broadcast_to(jnp.mean(o * o, axis=-1, keepdims=True), o.shape), out)
        hb_ref[rs, sl] = (out * lax.rsqrt(ms + HEAD_NORM_EPS) * og_all[rs, sl]).astype(hb_ref.dtype)

    return unit, [(sc, gp) for sc in range(L // S) for gp in range(H_HEADS // 2)]


def _mixer_kernel(layer, n_chunks, xcur_ref, xnext_ref, wb_ref, bb_ref, wf_ref, bf_ref, wgt_ref, bgt_ref,
                  cw_ref, cb_ref, mng_ref, lbl_ref, hng_ref, ha_ref, hb_ref,
                  ub_ref, ug_ref, gr_ref, c_ref, n_ref, m_ref, tail_ref, st_ref):
    L = xcur_ref.shape[0]
    nv = L // SUBLANES
    k = pl.program_id(0)
    rr = lax.broadcasted_iota(jnp.int32, (L, L), 0)
    cc = lax.broadcasted_iota(jnp.int32, (L, L), 1)
    t_r = (rr % SUBLANES) * nv + rr // SUBLANES
    t_c = (cc % SUBLANES) * nv + cc // SUBLANES
    to_permuted = lambda x: _dot((cc == t_r).astype(BF16), x).astype(BF16)

    def add_rows(acc, b8):
        n = acc.shape[1]
        return (acc.reshape(L // SUBLANES, SUBLANES, n) + b8[None]).reshape(L, n)

    def gate_pieces(x_perm):
        def gates():
            ug_ref[...] = add_rows(_dot(x_perm, wf_ref[...]), bf_ref[...])
            gr_ref[...] = _dot_nt(wgt_ref[...], x_perm) + bgt_ref[...]
        return [gates]

    def head_pieces(x_perm, h):
        def piece(c):
            cs = slice(c, c + PIECE_COLS)
            ub_ref[:, cs] = add_rows(_dot(x_perm, wb_ref[:, cs]), bb_ref[:, cs]).astype(BF16)
        return [functools.partial(piece, c)
                for c in range(h * HEAD_COLS, (h + 1) * HEAD_COLS, PIECE_COLS)]

    def filler(pieces):
        it = iter(pieces)

        def fill(drain=False):
            for thunk in it:
                thunk()
                if not drain:
                    break
        return fill

    @pl.when(k == 0)
    def _():
        xp0 = to_permuted(xcur_ref[...])
        for thunk in gate_pieces(xp0) + [p for h in range(M_HEADS) for p in head_pieces(xp0, h)]:
            thunk()

    @pl.when(lax.rem(k, n_chunks) == 0)
    def _():
        c_ref[...] = jnp.zeros_like(c_ref)
        n_ref[...] = jnp.zeros_like(n_ref)
        m_ref[...] = jnp.zeros_like(m_ref)
        tail_ref[...] = jnp.zeros_like(tail_ref)
        st_ref[...] = jnp.zeros_like(st_ref)

    xpn = to_permuted(xnext_ref[...])
    gcol = ug_ref[...]
    grow = gr_ref[...]
    filler(gate_pieces(xpn))(drain=True)
    causal = t_c <= t_r
    tri = causal.astype(BF16)
    tri_t = (t_r <= t_c).astype(BF16)
    lf_hi, lf_lo = _split_bf16(_log_sigmoid(gcol))
    b_col = _dot(tri, lf_hi) + _dot(tri, lf_lo)
    lr_hi, lr_lo = _split_bf16(_log_sigmoid(grow))
    b_row = _dot(lr_hi, tri_t) + _dot(lr_lo, tri_t)

    hgrn_proj = add_rows(_dot(xcur_ref[...], wb_ref[:, COL_HQ:]), bb_ref[:, COL_HQ:])
    hgrn_unit, units = _hgrn_chunk(layer, hgrn_proj, lbl_ref, hng_ref, st_ref, hb_ref)
    units_per_head = -(-len(units) // M_HEADS)

    def after_head(h, fill):
        for sc, g in units[h * units_per_head:(h + 1) * units_per_head]:
            fill()
            hgrn_unit(sc, g)
        fill(drain=True)

    h_perm = _mlstm_chunk(ub_ref, causal, gcol, grow, b_col, b_row, cw_ref, cb_ref, mng_ref,
                          c_ref, n_ref, m_ref, tail_ref, lambda h: filler(head_pieces(xpn, h)), after_head)
    ha_ref[...] = _dot((rr == t_c).astype(BF16), h_perm).astype(ha_ref.dtype)


def _mixer(layer, xb3, wb, bb, wf, bf, wgt, bgt, conv_w, conv_b, m_norm_g, lb_logits, h_norm_g):
    bsz, s, d = xb3.shape
    L = min(MIX_CHUNK, s)
    n_chunks = s // L
    steps = bsz * n_chunks
    xk = xb3.reshape(steps, L, d)
    resident = lambda a: pl.BlockSpec(a.shape, lambda k: (0,) * a.ndim, pipeline_mode=pl.Buffered(1))
    rows = lambda w: pl.BlockSpec((None, L, w), lambda k: (k, 0, 0))
    next_rows = pl.BlockSpec((None, L, d), lambda k: (jnp.minimum(k + 1, steps - 1), 0, 0))
    consts = (wb, bb, wf, bf, wgt, bgt, conv_w, conv_b, m_norm_g, lb_logits, h_norm_g)
    return pl.pallas_call(
        functools.partial(_mixer_kernel, layer, n_chunks),
        grid=(steps,),
        in_specs=[rows(d), next_rows] + [resident(a) for a in consts],
        out_specs=[rows(M_WIDTH), rows(H_WIDTH)],
        out_shape=[jax.ShapeDtypeStruct((steps, L, M_WIDTH), BF16),
                   jax.ShapeDtypeStruct((steps, L, H_WIDTH), BF16)],
        scratch_shapes=[pltpu.VMEM((L, COL_HQ), BF16),
                        pltpu.VMEM((L, LANES), F32),
                        pltpu.VMEM((SUBLANES, L), F32),
                        pltpu.VMEM((M_HEADS, M_HEAD_DIM, M_HEAD_DIM), F32),
                        pltpu.VMEM((M_HEADS, 1, M_HEAD_DIM), F32),
                        pltpu.VMEM((SUBLANES, LANES), F32),
                        pltpu.VMEM(((CONV_WIDTH - 1) * SUBLANES, 2 * M_WIDTH), F32),
                        pltpu.VMEM((H_HEADS, H_DIM, H_DIM), F32)],
        compiler_params=_params("arbitrary"),
        name="mixer",
    )(xk, xk, *consts)


def _merge_kernel(alpha, ha_ref, hb_ref, x_ref, xb_ref, wgab_ref, bgab_ref, pa_ref, pb_ref, wo_ref,
                  g_ref, b_ref, o_ref, ob_ref):
    gates = _sigmoid(_dot(xb_ref[...], wgab_ref[...]) + bgab_ref[...])
    d = x_ref.shape[1]
    ya = _dot(ha_ref[...], pa_ref[...])
    yb = _dot(hb_ref[...], pb_ref[...])
    mixed = gates[:, :d] * ya + gates[:, d:] * yb
    y = alpha * x_ref[...] + _dot(mixed.astype(BF16), wo_ref[...])
    out = _layernorm_rows(y, g_ref[...], b_ref[...])
    o_ref[...] = out
    ob_ref[...] = out.astype(BF16)


def _merge(alpha, ha, hb, x, xb, wgab, bgab, pa, pb, wo, g, b):
    t, d = x.shape
    tm = min(MERGE_TM, t)
    rows = lambda w: pl.BlockSpec((tm, w), lambda i: (i, 0))
    full = lambda a: pl.BlockSpec(a.shape, lambda i: (0,) * a.ndim, pipeline_mode=pl.Buffered(1))
    consts = (wgab, bgab, pa, pb, wo, g, b)
    return pl.pallas_call(
        functools.partial(_merge_kernel, alpha),
        grid=(t // tm,),
        in_specs=[rows(M_WIDTH), rows(H_WIDTH), rows(d), rows(d)] + [full(a) for a in consts],
        out_specs=[rows(d), rows(d)],
        out_shape=[jax.ShapeDtypeStruct((t, d), F32), jax.ShapeDtypeStruct((t, d), BF16)],
        compiler_params=_params("parallel"),
        name="merge_ln",
    )(ha, hb, x, xb, *consts)


def _ffn_kernel(alpha, x_ref, xb_ref, wg_ref, wu_ref, wd_ref, g_ref, b_ref, o_ref, ob_ref):
    xb = xb_ref[...]
    gate = _dot(xb, wg_ref[...])
    up = _dot(xb, wu_ref[...])
    hid = (_silu(gate) * up).astype(BF16)
    y = alpha * x_ref[...] + _dot(hid, wd_ref[...])
    out = _layernorm_rows(y, g_ref[...], b_ref[...])
    o_ref[...] = out
    ob_ref[...] = out.astype(BF16)


def _ffn(alpha, x, xb, wg, wu, wd, g, b):
    t, d = x.shape
    tm = min(FFN_TM, t)
    rows = pl.BlockSpec((tm, d), lambda i: (i, 0))
    full = lambda a: pl.BlockSpec(a.shape, lambda i: (0,) * a.ndim, pipeline_mode=pl.Buffered(1))
    return pl.pallas_call(
        functools.partial(_ffn_kernel, alpha),
        grid=(t // tm,),
        in_specs=[rows, rows, full(wg), full(wu), full(wd), full(g), full(b)],
        out_specs=[rows, rows],
        out_shape=[jax.ShapeDtypeStruct((t, d), F32), jax.ShapeDtypeStruct((t, d), BF16)],
        compiler_params=_params("parallel"),
        name="ffn_ln",
    )(x, xb, wg, wu, wd, g, b)


def _in_proj_layout(w_in, b_in):
    sizes = (M_WIDTH,) * 4 + (M_HEADS, M_HEADS) + (H_WIDTH,) * 6
    off = [int(o) for o in np.cumsum((0,) + sizes)]
    seg = lambda a, i: a[..., off[i]:off[i + 1]]
    rows8 = lambda b: jnp.broadcast_to(b[:, None, :], (b.shape[0], SUBLANES, b.shape[1]))
    dh = M_HEAD_DIM

    def main_group(a):
        heads = [seg(a, i)[..., h * dh:(h + 1) * dh] for h in range(M_HEADS) for i in (0, 1, 2, 3)]
        return jnp.concatenate(heads + [seg(a, i) for i in (6, 8, 9, 7)], axis=-1)

    wb = main_group(w_in).astype(BF16)
    bb = rows8(main_group(b_in))
    wgab = jnp.concatenate([seg(w_in, 10), seg(w_in, 11)], axis=-1).astype(BF16)
    bgab = jnp.concatenate([seg(b_in, 10), seg(b_in, 11)], axis=-1)[:, None, :]
    pad = LANES - 2 * M_HEADS
    wf = jnp.concatenate([seg(w_in, 4), seg(w_in, 5),
                          jnp.zeros(w_in.shape[:2] + (pad,), w_in.dtype)], axis=-1).astype(BF16)
    bf = rows8(jnp.concatenate([seg(b_in, 4), seg(b_in, 5),
                                jnp.zeros(b_in.shape[:1] + (pad,), b_in.dtype)], axis=-1))
    wgt = jnp.swapaxes(jnp.concatenate([seg(w_in, 4), seg(w_in, 5)], axis=-1), 1, 2).astype(BF16)
    bgt = jnp.concatenate([seg(b_in, 4), seg(b_in, 5)], axis=-1)[:, :, None]
    return wb, bb, wf, bf, wgt, bgt, wgab, bgab


def kernel(x, w_in, b_in, conv_w, conv_b, m_norm_g, lb_logits, h_norm_g, w_proj_a, w_proj_b, w_out,
           ln1_g, ln1_b, w_ffn_gate, w_ffn_up, w_ffn_down, ln2_g, ln2_b):
    bsz, s, d = x.shape
    depth = w_in.shape[0]
    alpha = float((2 * depth) ** 0.25)
    t = bsz * s

    wb, bb, wf, bf, wgt, bgt, wgab, bgab = _in_proj_layout(w_in, b_in)
    pa = w_proj_a.astype(BF16)
    pb = w_proj_b.astype(BF16)
    wo = w_out.astype(BF16)
    wg = w_ffn_gate.astype(BF16)
    wu = w_ffn_up.astype(BF16)
    wd = w_ffn_down.astype(BF16)

    xf = x.reshape(t, d)
    xb = xf.astype(BF16)
    for layer in range(depth):
        ha, hb = _mixer(layer, xb.reshape(bsz, s, d), wb[layer], bb[layer], wf[layer], bf[layer],
                        wgt[layer], bgt[layer], conv_w[layer], conv_b[layer][None, :],
                        m_norm_g[layer][None, :], lb_logits, h_norm_g[layer][None, :])
        xf, xb = _merge(alpha, ha.reshape(t, M_WIDTH), hb.reshape(t, H_WIDTH), xf, xb,
                        wgab[layer], bgab[layer], pa[layer], pb[layer], wo[layer],
                        ln1_g[layer][None, :], ln1_b[layer][None, :])
        xf, xb = _ffn(alpha, xf, xb, wg[layer], wu[layer], wd[layer],
                      ln2_g[layer][None, :], ln2_b[layer][None, :])
    return xf.reshape(bsz, s, d)
```

```python
import functools

import numpy as np
import jax
import jax.numpy as jnp
from jax import lax
from jax.experimental import pallas as pl
from jax.experimental.pallas import tpu as pltpu

F32 = jnp.float32
BF16 = jnp.bfloat16

M_HEADS = 4
M_HEAD_DIM = 512
M_WIDTH = M_HEADS * M_HEAD_DIM
CONV_WIDTH = 4
H_HEADS = 8
H_DIM = 128
H_WIDTH = H_HEADS * H_DIM
LN_EPS = 1e-5
HEAD_NORM_EPS = 1e-6

SUBLANES = 8
LANES = 128
VMEM_LIMIT_BYTES = 60 * 1024 * 1024

MIX_CHUNK = 256
H_CHUNK = 128
MERGE_TM = 512
FFN_TM = 512

HEAD_COLS = 4 * M_HEAD_DIM
COL_HQ = M_HEADS * HEAD_COLS
PIECE_COLS = 256


def _sigmoid(x):
    return 1.0 / (1.0 + jnp.exp(-x))


def _silu(x):
    return x * _sigmoid(x)


def _log_sigmoid(x):
    return jnp.minimum(x, 0.0) - jnp.log1p(jnp.exp(-jnp.abs(x)))


def _split_bf16(x):
    hi = x.astype(BF16)
    lo = (x - hi.astype(F32)).astype(BF16)
    return hi, lo


def _dot(a, b):
    return jnp.dot(a, b, preferred_element_type=F32)


def _dot_nt(a, b):
    return lax.dot_general(a, b, (((1,), (1,)), ((), ())), preferred_element_type=F32)


def _dot_tn(a, b):
    return lax.dot_general(a, b, (((0,), (0,)), ((), ())), preferred_element_type=F32)


def _layernorm_rows(y, g, b):
    mu = jnp.mean(y, axis=-1, keepdims=True)
    d = y - mu
    var = jnp.mean(d * d, axis=-1, keepdims=True)
    return d * lax.rsqrt(var + LN_EPS) * g + b


def _params(*semantics):
    return pltpu.CompilerParams(dimension_semantics=semantics, vmem_limit_bytes=VMEM_LIMIT_BYTES)


def _conv_silu_permuted(pre, tail, w, bias):
    rows = pre.shape[0]
    sub = lax.broadcasted_iota(jnp.int32, (SUBLANES, pre.shape[1]), 0)
    acc = pre * w[CONV_WIDTH - 1:CONV_WIDTH, :] + bias
    for d in range(1, CONV_WIDTH):
        cur = pre[rows - d * SUBLANES:, :]
        prv = tail[(CONV_WIDTH - 1 - d) * SUBLANES:, :]
        tiles = []
        for i in range(d):
            ts = slice(i * SUBLANES, (i + 1) * SUBLANES)
            tiles.append(jnp.where(sub == 0, pltpu.roll(prv[ts], 1, 0), pltpu.roll(cur[ts], 1, 0)))
        shifted = jnp.concatenate(tiles + [pre[:rows - d * SUBLANES, :]], axis=0)
        acc = acc + shifted * w[CONV_WIDTH - 1 - d:CONV_WIDTH - d, :]
    return _silu(acc)


def _mlstm_chunk(ub_ref, causal, gcol, grow, b_col, b_row, cw_ref, cb_ref, ng_ref,
                 c_ref, n_ref, m_ref, tail_ref, head_filler, after_head):
    L = ub_ref.shape[0]
    dh = M_HEAD_DIM
    log_scale = float(np.log(dh ** -0.5))
    tail_rows = (CONV_WIDTH - 1) * SUBLANES
    outs = []
    for h in range(M_HEADS):
        sl = slice(h * dh, (h + 1) * dh)
        slk = slice(M_WIDTH + h * dh, M_WIDTH + (h + 1) * dh)
        c0 = h * HEAD_COLS
        q_pre = ub_ref[:, c0:c0 + dh].astype(F32)
        k_pre = ub_ref[:, c0 + dh:c0 + 2 * dh].astype(F32)
        v = ub_ref[:, c0 + 2 * dh:c0 + 3 * dh]
        o_gate = ub_ref[:, c0 + 3 * dh:c0 + 4 * dh]
        fill = head_filler(h)

        fill()
        q = _conv_silu_permuted(q_pre, tail_ref[:, sl], cw_ref[:, sl], cb_ref[:, sl])
        fill()
        k = _conv_silu_permuted(k_pre, tail_ref[:, slk], cw_ref[:, slk], cb_ref[:, slk])
        tail_ref[:, sl] = q_pre[L - tail_rows:, :]
        tail_ref[:, slk] = k_pre[L - tail_rows:, :]
        qb = q.astype(BF16)
        kb = k.astype(BF16)
        fill()

        m_prev = m_ref[h:h + 1, 0:1]
        r_row = grow[h:h + 1, :] - b_row[M_HEADS + h:M_HEADS + h + 1, :]
        bh_col = b_col[:, M_HEADS + h:M_HEADS + h + 1]
        r_col = gcol[:, h:h + 1] - bh_col
        big_m = jnp.maximum(jnp.max(jnp.where(causal, r_row, -jnp.inf), axis=-1, keepdims=True),
                            m_prev)
        decay_w = jnp.exp(jnp.where(causal, (r_row + log_scale) - big_m, -jnp.inf))
        s_inter = jnp.exp(m_prev - big_m)
        floor = jnp.exp(-(bh_col + big_m))

        w = decay_w * _dot_nt(qb, kb)
        fill()
        num = _dot(w.astype(BF16), v) + s_inter * _dot(qb, c_ref[h].astype(BF16))
        den = (jnp.sum(w, axis=-1, keepdims=True)
               + s_inter * jnp.sum(q * n_ref[h], axis=-1, keepdims=True))
        hh = num / jnp.maximum(jnp.abs(den), floor)
        fill()

        m_end = big_m[L - 1:L, :]
        decay = jnp.exp(m_prev - m_end)
        gk = jnp.exp((r_col + log_scale) - m_end) * k
        c_ref[h] = decay * c_ref[h] + _dot_tn(gk.astype(BF16), v)
        n_ref[h] = decay * n_ref[h] + jnp.sum(gk, axis=0, keepdims=True)
        m_ref[h:h + 1, :] = jnp.broadcast_to(bh_col[L - 1:L, :] + m_end, (1, LANES))
        fill()

        mu = jnp.mean(hh, axis=-1, keepdims=True)
        dlt = hh - mu
        var = jnp.mean(dlt * dlt, axis=-1, keepdims=True)
        hn = dlt * lax.rsqrt(var + HEAD_NORM_EPS) * ng_ref[:, sl]
        outs.append((hn * _sigmoid(o_gate.astype(F32))).astype(BF16))
        after_head(h, fill)
    return jnp.concatenate(outs, axis=-1)


def _block_ref_rows(a, h):
    L, n = a.shape
    if h >= SUBLANES:
        a4 = a.reshape(L // (2 * h), 2 * h, n)
        return jnp.broadcast_to(a4[:, h - 1:h, :], a4.shape).reshape(L, n)
    a3 = a.reshape(L // SUBLANES, SUBLANES, n)
    sub = lax.broadcasted_iota(jnp.int32, a3.shape, 1)
    if h == 4:
        ref = jnp.broadcast_to(a3[:, 3:4, :], a3.shape)
    elif h == 2:
        ref = jnp.where(sub < 4, jnp.broadcast_to(a3[:, 1:2, :], a3.shape),
                        jnp.broadcast_to(a3[:, 5:6, :], a3.shape))
    else:
        ref = jnp.where(sub % 2 == 1, pltpu.roll(a, 1, 0).reshape(a3.shape), a3)
    return ref.reshape(L, n)


def _hgrn_chunk(layer, uh_ref, uhf_ref, lbl_ref, ng_ref, st_ref, hb_ref):
    L = uh_ref.shape[0]
    S = min(H_CHUNK, L)
    levels = [1 << i for i in range(S.bit_length() - 1)]
    hq = uh_ref[:, :H_WIDTH].astype(F32)
    v_all = uh_ref[:, H_WIDTH:2 * H_WIDTH]
    hg = uh_ref[:, 2 * H_WIDTH:].astype(F32)
    hf = uhf_ref[...]

    logits = lbl_ref[...]
    p = jnp.exp(logits - jnp.max(logits, axis=0, keepdims=True))
    p = p / jnp.sum(p, axis=0, keepdims=True)
    lb = jnp.zeros((1, H_WIDTH), F32)
    for d in range(1, layer + 1):
        lb = lb + p[d:d + 1, :]
    softplus_tail = jnp.log(1.0 + jnp.exp(-jnp.abs(hf)))
    log1m_lb = jnp.log1p(-lb)
    x2 = log1m_lb + jnp.minimum(hf, 0.0) - softplus_tail
    if layer == 0:
        lf = x2
    else:
        x1 = jnp.log(lb)
        lf = jnp.maximum(x1, x2) + jnp.log(1.0 + jnp.exp(-jnp.abs(x1 - x2)))
    kk_all = jnp.exp(log1m_lb - jnp.maximum(hf, 0.0) - softplus_tail)
    qq_all = _silu(hq)
    og_all = _sigmoid(hg) * ng_ref[...]

    rl = lax.broadcasted_iota(jnp.int32, (L, L), 0)
    cl = lax.broadcasted_iota(jnp.int32, (L, L), 1)
    tri = ((cl <= rl) & (rl // S == cl // S)).astype(BF16)
    lf_hi, lf_lo = _split_bf16(lf)
    a_all = _dot(tri, lf_hi) + _dot(tri, lf_lo)

    D = H_DIM
    row = lax.broadcasted_iota(jnp.int32, (S, 2 * S), 0)
    col = lax.broadcasted_iota(jnp.int32, (S, 2 * S), 1) % S
    trow = lax.broadcasted_iota(jnp.int32, (S, 1), 0)
    upper = [(trow // h) % 2 == 1 for h in levels]
    pair = [(row // (2 * h) == col // (2 * h)) & ((row // h) % 2 == 1) & ((col // h) % 2 == 0)
            for h in levels]
    first = lax.broadcasted_iota(jnp.int32, (S, 2 * S), 1) < S

    def block_diag(x):
        z = jnp.zeros((x.shape[0], D), x.dtype)
        return jnp.concatenate([jnp.concatenate([x[:, :D], z], axis=1),
                                jnp.concatenate([z, x[:, D:]], axis=1)], axis=0)

    def block_diag_cols(xt):
        z = jnp.zeros((D, xt.shape[1]), xt.dtype)
        return jnp.concatenate([jnp.concatenate([xt[:D], z], axis=1),
                                jnp.concatenate([z, xt[D:]], axis=1)], axis=0)

    def per_head(fn, x):
        return jnp.concatenate([fn(x[:, :D]), fn(x[:, D:])], axis=1)

    def unit(sc, gp):
        rs = slice(sc * S, (sc + 1) * S)
        sl = slice(2 * gp * D, (2 * gp + 2) * D)
        a = a_all[rs, sl]
        kk = kk_all[rs, sl]
        qq = qq_all[rs, sl]
        vb = v_all[rs, sl]

        qk = qq * kk
        diag = jnp.where(first, jnp.sum(qk[:, :D], axis=-1, keepdims=True),
                         jnp.sum(qk[:, D:], axis=-1, keepdims=True))
        att = jnp.where(row == col, diag, 0.0)
        for h, up, pr in zip(levels, upper, pair):
            fac = jnp.exp(-jnp.abs(a - _block_ref_rows(a, h)))
            pm = (jnp.where(up, qq, kk) * fac).astype(BF16)
            att = jnp.where(pr, _dot(pm, block_diag_cols(pm.T)), att)

        st = jnp.concatenate([st_ref[2 * gp], st_ref[2 * gp + 1]], axis=1)
        out = (_dot(att.astype(BF16), block_diag(vb))
               + _dot_nt((qq * jnp.exp(a)).astype(BF16), block_diag(st.astype(BF16))))
        a_end = a[S - 1:S, :]
        upd = _dot_tn(vb, (kk * jnp.exp(a_end - a)).astype(BF16))
        st_new = jnp.exp(a_end) * st + jnp.concatenate([upd[:D, :D], upd[D:, D:]], axis=1)
        st_ref[2 * gp] = st_new[:, :D]
        st_ref[2 * gp + 1] = st_new[:, D:]

        ms = per_head(lambda o: jnp.broadcast_to(jnp.mean(o * o, axis=-1, keepdims=True), o.shape), out)
        hb_ref[rs, sl] = (out * lax.rsqrt(ms + HEAD_NORM_EPS) * og_all[rs, sl]).astype(hb_ref.dtype)

    return unit, [(sc, gp) for sc in range(L // S) for gp in range(H_HEADS // 2)]


def _mixer_kernel(layer, n_chunks, xcur_ref, xnext_ref, wb_ref, bb_ref, wf_ref, bf_ref, wgt_ref, bgt_ref,
                  cw_ref, cb_ref, mng_ref, lbl_ref, hng_ref, ha_ref, hb_ref,
                  ub_ref, ug_ref, gr_ref, uh_ref, uhf_ref, c_ref, n_ref, m_ref, tail_ref, st_ref):
    L = xcur_ref.shape[0]
    nv = L // SUBLANES
    k = pl.program_id(0)
    rr = lax.broadcasted_iota(jnp.int32, (L, L), 0)
    cc = lax.broadcasted_iota(jnp.int32, (L, L), 1)
    t_r = (rr % SUBLANES) * nv + rr // SUBLANES
    t_c = (cc % SUBLANES) * nv + cc // SUBLANES
    to_permuted = lambda x: _dot((cc == t_r).astype(BF16), x).astype(BF16)

    def add_rows(acc, b8):
        n = acc.shape[1]
        return (acc.reshape(L // SUBLANES, SUBLANES, n) + b8[None]).reshape(L, n)

    def gate_pieces(x_perm):
        def gates():
            ug_ref[...] = add_rows(_dot(x_perm, wf_ref[...]), bf_ref[...])
            gr_ref[...] = _dot_nt(wgt_ref[...], x_perm) + bgt_ref[...]
        return [gates]

    def head_pieces(x_perm, h):
        def piece(c):
            cs = slice(c, c + PIECE_COLS)
            ub_ref[:, cs] = add_rows(_dot(x_perm, wb_ref[:, cs]), bb_ref[:, cs]).astype(BF16)
        return [functools.partial(piece, c)
                for c in range(h * HEAD_COLS, (h + 1) * HEAD_COLS, PIECE_COLS)]

    def hgrn_pieces(x_nat):
        def piece(c):
            cs = slice(COL_HQ + c, COL_HQ + c + PIECE_COLS)
            acc = add_rows(_dot(x_nat, wb_ref[:, cs]), bb_ref[:, cs])
            if c < 3 * H_WIDTH:
                uh_ref[:, c:c + PIECE_COLS] = acc.astype(BF16)
            else:
                uhf_ref[:, c - 3 * H_WIDTH:c - 3 * H_WIDTH + PIECE_COLS] = acc
        return [functools.partial(piece, c) for c in range(0, 4 * H_WIDTH, PIECE_COLS)]

    def filler(pieces):
        it = iter(pieces)

        def fill(drain=False):
            for thunk in it:
                thunk()
                if not drain:
                    break
        return fill

    @pl.when(k == 0)
    def _():
        x0 = xcur_ref[...]
        xp0 = to_permuted(x0)
        for thunk in (gate_pieces(xp0) + [p for h in range(M_HEADS) for p in head_pieces(xp0, h)]
                      + hgrn_pieces(x0)):
            thunk()

    @pl.when(lax.rem(k, n_chunks) == 0)
    def _():
        c_ref[...] = jnp.zeros_like(c_ref)
        n_ref[...] = jnp.zeros_like(n_ref)
        m_ref[...] = jnp.zeros_like(m_ref)
        tail_ref[...] = jnp.zeros_like(tail_ref)
        st_ref[...] = jnp.zeros_like(st_ref)

    xn = xnext_ref[...]
    xpn = to_permuted(xn)
    gcol = ug_ref[...]
    grow = gr_ref[...]
    filler(gate_pieces(xpn))(drain=True)
    causal = t_c <= t_r
    tri = causal.astype(BF16)
    tri_t = (t_r <= t_c).astype(BF16)
    lf_hi, lf_lo = _split_bf16(_log_sigmoid(gcol))
    b_col = _dot(tri, lf_hi) + _dot(tri, lf_lo)
    lr_hi, lr_lo = _split_bf16(_log_sigmoid(grow))
    b_row = _dot(lr_hi, tri_t) + _dot(lr_lo, tri_t)

    hgrn_unit, units = _hgrn_chunk(layer, uh_ref, uhf_ref, lbl_ref, hng_ref, st_ref, hb_ref)
    units_per_head = -(-len(units) // M_HEADS)
    next_hgrn = hgrn_pieces(xn)
    hgrn_per_head = -(-len(next_hgrn) // M_HEADS)

    def head_filler(h):
        mine = head_pieces(xpn, h)
        ride = next_hgrn[h * hgrn_per_head:(h + 1) * hgrn_per_head]
        mixed = [p for pair in zip(mine, ride) for p in pair] + mine[len(ride):] + ride[len(mine):]
        return filler(mixed)

    def after_head(h, fill):
        for sc, g in units[h * units_per_head:(h + 1) * units_per_head]:
            fill()
            fill()
            hgrn_unit(sc, g)
        fill(drain=True)

    h_perm = _mlstm_chunk(ub_ref, causal, gcol, grow, b_col, b_row, cw_ref, cb_ref, mng_ref,
                          c_ref, n_ref, m_ref, tail_ref, head_filler, after_head)
    ha_ref[...] = _dot((rr == t_c).astype(BF16), h_perm).astype(ha_ref.dtype)


def _mixer(layer, xb3, wb, bb, wf, bf, wgt, bgt, conv_w, conv_b, m_norm_g, lb_logits, h_norm_g):
    bsz, s, d = xb3.shape
    L = min(MIX_CHUNK, s)
    n_chunks = s // L
    steps = bsz * n_chunks
    xk = xb3.reshape(steps, L, d)
    resident = lambda a: pl.BlockSpec(a.shape, lambda k: (0,) * a.ndim, pipeline_mode=pl.Buffered(1))
    rows = lambda w: pl.BlockSpec((None, L, w), lambda k: (k, 0, 0))
    next_rows = pl.BlockSpec((None, L, d), lambda k: (jnp.minimum(k + 1, steps - 1), 0, 0))
    consts = (wb, bb, wf, bf, wgt, bgt, conv_w, conv_b, m_norm_g, lb_logits, h_norm_g)
    return pl.pallas_call(
        functools.partial(_mixer_kernel, layer, n_chunks),
        grid=(steps,),
        in_specs=[rows(d), next_rows] + [resident(a) for a in consts],
        out_specs=[rows(M_WIDTH), rows(H_WIDTH)],
        out_shape=[jax.ShapeDtypeStruct((steps, L, M_WIDTH), BF16),
                   jax.ShapeDtypeStruct((steps, L, H_WIDTH), BF16)],
        scratch_shapes=[pltpu.VMEM((L, COL_HQ), BF16),
                        pltpu.VMEM((L, LANES), F32),
                        pltpu.VMEM((SUBLANES, L), F32),
                        pltpu.VMEM((L, 3 * H_WIDTH), BF16),
                        pltpu.VMEM((L, H_WIDTH), F32),
                        pltpu.VMEM((M_HEADS, M_HEAD_DIM, M_HEAD_DIM), F32),
                        pltpu.VMEM((M_HEADS, 1, M_HEAD_DIM), F32),
                        pltpu.VMEM((SUBLANES, LANES), F32),
                        pltpu.VMEM(((CONV_WIDTH - 1) * SUBLANES, 2 * M_WIDTH), F32),
                        pltpu.VMEM((H_HEADS, H_DIM, H_DIM), F32)],
        compiler_params=_params("arbitrary"),
        name="mixer",
    )(xk, xk, *consts)


def _merge_kernel(alpha, ha_ref, hb_ref, x_ref, xb_ref, wgab_ref, bgab_ref, pa_ref, pb_ref, wo_ref,
                  g_ref, b_ref, o_ref, ob_ref):
    gates = _sigmoid(_dot(xb_ref[...], wgab_ref[...]) + bgab_ref[...])
    d = x_ref.shape[1]
    ya = _dot(ha_ref[...], pa_ref[...])
    yb = _dot(hb_ref[...], pb_ref[...])
    mixed = gates[:, :d] * ya + gates[:, d:] * yb
    y = alpha * x_ref[...] + _dot(mixed.astype(BF16), wo_ref[...])
    out = _layernorm_rows(y, g_ref[...], b_ref[...])
    o_ref[...] = out
    ob_ref[...] = out.astype(BF16)


def _merge(alpha, ha, hb, x, xb, wgab, bgab, pa, pb, wo, g, b):
    t, d = x.shape
    tm = min(MERGE_TM, t)
    rows = lambda w: pl.BlockSpec((tm, w), lambda i: (i, 0))
    full = lambda a: pl.BlockSpec(a.shape, lambda i: (0,) * a.ndim, pipeline_mode=pl.Buffered(1))
    consts = (wgab, bgab, pa, pb, wo, g, b)
    return pl.pallas_call(
        functools.partial(_merge_kernel, alpha),
        grid=(t // tm,),
        in_specs=[rows(M_WIDTH), rows(H_WIDTH), rows(d), rows(d)] + [full(a) for a in consts],
        out_specs=[rows(d), rows(d)],
        out_shape=[jax.ShapeDtypeStruct((t, d), F32), jax.ShapeDtypeStruct((t, d), BF16)],
        compiler_params=_params("parallel"),
        name="merge_ln",
    )(ha, hb, x, xb, *consts)


def _ffn_kernel(alpha, x_ref, xb_ref, wg_ref, wu_ref, wd_ref, g_ref, b_ref, o_ref, ob_ref):
    xb = xb_ref[...]
    gate = _dot(xb, wg_ref[...])
    up = _dot(xb, wu_ref[...])
    hid = (_silu(gate) * up).astype(BF16)
    y = alpha * x_ref[...] + _dot(hid, wd_ref[...])
    out = _layernorm_rows(y, g_ref[...], b_ref[...])
    o_ref[...] = out
    ob_ref[...] = out.astype(BF16)


def _ffn(alpha, x, xb, wg, wu, wd, g, b):
    t, d = x.shape
    tm = min(FFN_TM, t)
    rows = pl.BlockSpec((tm, d), lambda i: (i, 0))
    full = lambda a: pl.BlockSpec(a.shape, lambda i: (0,) * a.ndim, pipeline_mode=pl.Buffered(1))
    return pl.pallas_call(
        functools.partial(_ffn_kernel, alpha),
        grid=(t // tm,),
        in_specs=[rows, rows, full(wg), full(wu), full(wd), full(g), full(b)],
        out_specs=[rows, rows],
        out_shape=[jax.ShapeDtypeStruct((t, d), F32), jax.ShapeDtypeStruct((t, d), BF16)],
        compiler_params=_params("parallel"),
        name="ffn_ln",
    )(x, xb, wg, wu, wd, g, b)


def _in_proj_layout(w_in, b_in):
    sizes = (M_WIDTH,) * 4 + (M_HEADS, M_HEADS) + (H_WIDTH,) * 6
    off = [int(o) for o in np.cumsum((0,) + sizes)]
    seg = lambda a, i: a[..., off[i]:off[i + 1]]
    rows8 = lambda b: jnp.broadcast_to(b[:, None, :], (b.shape[0], SUBLANES, b.shape[1]))
    dh = M_HEAD_DIM

    def main_group(a):
        heads = [seg(a, i)[..., h * dh:(h + 1) * dh] for h in range(M_HEADS) for i in (0, 1, 2, 3)]
        return jnp.concatenate(heads + [seg(a, i) for i in (6, 8, 9, 7)], axis=-1)

    wb = main_group(w_in).astype(BF16)
    bb = rows8(main_group(b_in))
    wgab = jnp.concatenate([seg(w_in, 10), seg(w_in, 11)], axis=-1).astype(BF16)
    bgab = jnp.concatenate([seg(b_in, 10), seg(b_in, 11)], axis=-1)[:, None, :]
    pad = LANES - 2 * M_HEADS
    wf = jnp.concatenate([seg(w_in, 4), seg(w_in, 5),
                          jnp.zeros(w_in.shape[:2] + (pad,), w_in.dtype)], axis=-1).astype(BF16)
    bf = rows8(jnp.concatenate([seg(b_in, 4), seg(b_in, 5),
                                jnp.zeros(b_in.shape[:1] + (pad,), b_in.dtype)], axis=-1))
    wgt = jnp.swapaxes(jnp.concatenate([seg(w_in, 4), seg(w_in, 5)], axis=-1), 1, 2).astype(BF16)
    bgt = jnp.concatenate([seg(b_in, 4), seg(b_in, 5)], axis=-1)[:, :, None]
    return wb, bb, wf, bf, wgt, bgt, wgab, bgab


def kernel(x, w_in, b_in, conv_w, conv_b, m_norm_g, lb_logits, h_norm_g, w_proj_a, w_proj_b, w_out,
           ln1_g, ln1_b, w_ffn_gate, w_ffn_up, w_ffn_down, ln2_g, ln2_b):
    bsz, s, d = x.shape
    depth = w_in.shape[0]
    alpha = float((2 * depth) ** 0.25)
    t = bsz * s

    wb, bb, wf, bf, wgt, bgt, wgab, bgab = _in_proj_layout(w_in, b_in)
    pa = w_proj_a.astype(BF16)
    pb = w_proj_b.astype(BF16)
    wo = w_out.astype(BF16)
    wg = w_ffn_gate.astype(BF16)
    wu = w_ffn_up.astype(BF16)
    wd = w_ffn_down.astype(BF16)

    xf = x.reshape(t, d)
    xb = xf.astype(BF16)
    for layer in range(depth):
        ha, hb = _mixer(layer, xb.reshape(bsz, s, d), wb[layer], bb[layer], wf[layer], bf[layer],
                        wgt[layer], bgt[layer], conv_w[layer], conv_b[layer][None, :],
                        m_norm_g[layer][None, :], lb_logits, h_norm_g[layer][None, :])
        xf, xb = _merge(alpha, ha.reshape(t, M_WIDTH), hb.reshape(t, H_WIDTH), xf, xb,
                        wgab[layer], bgab[layer], pa[layer], pb[layer], wo[layer],
                        ln1_g[layer][None, :], ln1_b[layer][None, :])
        xf, xb = _ffn(alpha, xf, xb, wg[layer], wu[layer], wd[layer],
                      ln2_g[layer][None, :], ln2_b[layer][None, :])
    return xf.reshape(bsz, s, d)
```

```python
import functools

import numpy as np
import jax
import jax.numpy as jnp
from jax import lax
from jax.experimental import pallas as pl
from jax.experimental.pallas import tpu as pltpu

F32 = jnp.float32
BF16 = jnp.bfloat16

M_HEADS = 4
M_HEAD_DIM = 512
M_WIDTH = M_HEADS * M_HEAD_DIM
CONV_WIDTH = 4
H_HEADS = 8
H_DIM = 128
H_WIDTH = H_HEADS * H_DIM
LN_EPS = 1e-5
HEAD_NORM_EPS = 1e-6

SUBLANES = 8
LANES = 128
VMEM_LIMIT_BYTES = 60 * 1024 * 1024

MIX_CHUNK = 256
H_CHUNK = 128
MERGE_TM = 512
FFN_TM = 512

HEAD_COLS = 4 * M_HEAD_DIM
COL_HQ = M_HEADS * HEAD_COLS
PIECE_COLS = 256


NEG_LOG2_E = -1.4426950408889634


def _exp_neg(x):
    return jnp.exp2(x * NEG_LOG2_E)


def _neg_abs(x):
    bits = pltpu.bitcast(x, jnp.uint32) | jnp.uint32(0x80000000)
    return pltpu.bitcast(bits, F32)


def _sigmoid(x):
    return 1.0 / (1.0 + _exp_neg(x))


def _silu(x):
    return x * _sigmoid(x)


def _log_sigmoid(x):
    return jnp.minimum(x, 0.0) - jnp.log1p(_exp_neg(jnp.abs(x)))


def _split_bf16(x):
    hi = x.astype(BF16)
    lo = (x - hi.astype(F32)).astype(BF16)
    return hi, lo


def _dot(a, b):
    return jnp.dot(a, b, preferred_element_type=F32)


def _dot_nt(a, b):
    return lax.dot_general(a, b, (((1,), (1,)), ((), ())), preferred_element_type=F32)


def _dot_tn(a, b):
    return lax.dot_general(a, b, (((0,), (0,)), ((), ())), preferred_element_type=F32)


def _layernorm_rows(y, g, b):
    mu = jnp.mean(y, axis=-1, keepdims=True)
    d = y - mu
    var = jnp.mean(d * d, axis=-1, keepdims=True)
    return d * lax.rsqrt(var + LN_EPS) * g + b


def _params(*semantics):
    return pltpu.CompilerParams(dimension_semantics=semantics, vmem_limit_bytes=VMEM_LIMIT_BYTES)


def _conv_silu_permuted(pre, tail, w, bias):
    rows = pre.shape[0]
    sub = lax.broadcasted_iota(jnp.int32, (SUBLANES, pre.shape[1]), 0)
    acc = pre * w[CONV_WIDTH - 1:CONV_WIDTH, :] + bias
    for d in range(1, CONV_WIDTH):
        cur = pre[rows - d * SUBLANES:, :]
        prv = tail[(CONV_WIDTH - 1 - d) * SUBLANES:, :]
        tiles = []
        for i in range(d):
            ts = slice(i * SUBLANES, (i + 1) * SUBLANES)
            tiles.append(jnp.where(sub == 0, pltpu.roll(prv[ts], 1, 0), pltpu.roll(cur[ts], 1, 0)))
        shifted = jnp.concatenate(tiles + [pre[:rows - d * SUBLANES, :]], axis=0)
        acc = acc + shifted * w[CONV_WIDTH - 1 - d:CONV_WIDTH - d, :]
    return _silu(acc)


def _mlstm_chunk(ub_ref, causal, gcol, grow, b_col, b_row, cw_ref, cb_ref, ng_ref,
                 c_ref, n_ref, m_ref, tail_ref, head_filler, after_head):
    L = ub_ref.shape[0]
    dh = M_HEAD_DIM
    log_scale = float(np.log(dh ** -0.5))
    tail_rows = (CONV_WIDTH - 1) * SUBLANES
    outs = []
    for h in range(M_HEADS):
        sl = slice(h * dh, (h + 1) * dh)
        slk = slice(M_WIDTH + h * dh, M_WIDTH + (h + 1) * dh)
        c0 = h * HEAD_COLS
        q_pre = ub_ref[:, c0:c0 + dh].astype(F32)
        k_pre = ub_ref[:, c0 + dh:c0 + 2 * dh].astype(F32)
        v = ub_ref[:, c0 + 2 * dh:c0 + 3 * dh]
        o_gate = ub_ref[:, c0 + 3 * dh:c0 + 4 * dh]
        fill = head_filler(h)

        fill()
        q = _conv_silu_permuted(q_pre, tail_ref[:, sl], cw_ref[:, sl], cb_ref[:, sl])
        fill()
        k = _conv_silu_permuted(k_pre, tail_ref[:, slk], cw_ref[:, slk], cb_ref[:, slk])
        tail_ref[:, sl] = q_pre[L - tail_rows:, :]
        tail_ref[:, slk] = k_pre[L - tail_rows:, :]
        qb = q.astype(BF16)
        kb = k.astype(BF16)
        fill()

        m_prev = m_ref[h:h + 1, 0:1]
        r_row = grow[h:h + 1, :] - b_row[M_HEADS + h:M_HEADS + h + 1, :]
        bh_col = b_col[:, M_HEADS + h:M_HEADS + h + 1]
        r_col = gcol[:, h:h + 1] - bh_col
        big_m = jnp.maximum(jnp.max(jnp.where(causal, r_row, -jnp.inf), axis=-1, keepdims=True),
                            m_prev)
        decay_w = jnp.exp(jnp.where(causal, (r_row + log_scale) - big_m, -jnp.inf))
        s_inter = jnp.exp(m_prev - big_m)
        floor = _exp_neg(bh_col + big_m)

        w = decay_w * _dot_nt(qb, kb)
        fill()
        num = _dot(w.astype(BF16), v) + s_inter * _dot(qb, c_ref[h].astype(BF16))
        den = (jnp.sum(w, axis=-1, keepdims=True)
               + s_inter * jnp.sum(q * n_ref[h], axis=-1, keepdims=True))
        hh = num / jnp.maximum(jnp.abs(den), floor)
        fill()

        m_end = big_m[L - 1:L, :]
        decay = jnp.exp(m_prev - m_end)
        gk = jnp.exp((r_col + log_scale) - m_end) * k
        c_ref[h] = decay * c_ref[h] + _dot_tn(gk.astype(BF16), v)
        n_ref[h] = decay * n_ref[h] + jnp.sum(gk, axis=0, keepdims=True)
        m_ref[h:h + 1, :] = jnp.broadcast_to(bh_col[L - 1:L, :] + m_end, (1, LANES))
        fill()

        mu = jnp.mean(hh, axis=-1, keepdims=True)
        dlt = hh - mu
        var = jnp.mean(dlt * dlt, axis=-1, keepdims=True)
        hn = dlt * lax.rsqrt(var + HEAD_NORM_EPS) * ng_ref[:, sl]
        outs.append((hn * _sigmoid(o_gate.astype(F32))).astype(BF16))
        after_head(h, fill)
    return jnp.concatenate(outs, axis=-1)


def _block_ref_rows(a, h):
    L, n = a.shape
    if h >= SUBLANES:
        a4 = a.reshape(L // (2 * h), 2 * h, n)
        return jnp.broadcast_to(a4[:, h - 1:h, :], a4.shape).reshape(L, n)
    a3 = a.reshape(L // SUBLANES, SUBLANES, n)
    sub = lax.broadcasted_iota(jnp.int32, a3.shape, 1)
    if h == 4:
        ref = jnp.broadcast_to(a3[:, 3:4, :], a3.shape)
    elif h == 2:
        ref = jnp.where(sub < 4, jnp.broadcast_to(a3[:, 1:2, :], a3.shape),
                        jnp.broadcast_to(a3[:, 5:6, :], a3.shape))
    else:
        ref = jnp.where(sub % 2 == 1, pltpu.roll(a, 1, 0).reshape(a3.shape), a3)
    return ref.reshape(L, n)


def _hgrn_chunk(layer, uh_ref, uhf_ref, lbl_ref, ng_ref, st_ref, hb_ref):
    L = uh_ref.shape[0]
    S = min(H_CHUNK, L)
    levels = [1 << i for i in range(S.bit_length() - 1)]
    hq = uh_ref[:, :H_WIDTH].astype(F32)
    v_all = uh_ref[:, H_WIDTH:2 * H_WIDTH]
    hg = uh_ref[:, 2 * H_WIDTH:].astype(F32)
    hf = uhf_ref[...]

    logits = lbl_ref[...]
    p = jnp.exp(logits - jnp.max(logits, axis=0, keepdims=True))
    p = p / jnp.sum(p, axis=0, keepdims=True)
    lb = jnp.zeros((1, H_WIDTH), F32)
    for d in range(1, layer + 1):
        lb = lb + p[d:d + 1, :]
    softplus_tail = jnp.log(1.0 + _exp_neg(jnp.abs(hf)))
    log1m_lb = jnp.log1p(-lb)
    x2 = log1m_lb + jnp.minimum(hf, 0.0) - softplus_tail
    if layer == 0:
        lf = x2
    else:
        x1 = jnp.log(lb)
        lf = jnp.maximum(x1, x2) + jnp.log(1.0 + _exp_neg(jnp.abs(x1 - x2)))
    kk_all = jnp.exp(log1m_lb - jnp.maximum(hf, 0.0) - softplus_tail)
    qq_all = _silu(hq)
    og_all = _sigmoid(hg) * ng_ref[...]

    rl = lax.broadcasted_iota(jnp.int32, (L, L), 0)
    cl = lax.broadcasted_iota(jnp.int32, (L, L), 1)
    tri = ((cl <= rl) & (rl // S == cl // S)).astype(BF16)
    lf_hi, lf_lo = _split_bf16(lf * (-NEG_LOG2_E))
    a_all = _dot(tri, lf_hi) + _dot(tri, lf_lo)

    D = H_DIM
    row = lax.broadcasted_iota(jnp.int32, (S, 2 * S), 0)
    col = lax.broadcasted_iota(jnp.int32, (S, 2 * S), 1) % S
    trow = lax.broadcasted_iota(jnp.int32, (S, 1), 0)
    upper = [(trow // h) % 2 == 1 for h in levels]
    pair = [(row // (2 * h) == col // (2 * h)) & ((row // h) % 2 == 1) & ((col // h) % 2 == 0)
            for h in levels]
    first = lax.broadcasted_iota(jnp.int32, (S, 2 * S), 1) < S

    def block_diag(x):
        z = jnp.zeros((x.shape[0], D), x.dtype)
        return jnp.concatenate([jnp.concatenate([x[:, :D], z], axis=1),
                                jnp.concatenate([z, x[:, D:]], axis=1)], axis=0)

    def block_diag_cols(xt):
        z = jnp.zeros((D, xt.shape[1]), xt.dtype)
        return jnp.concatenate([jnp.concatenate([xt[:D], z], axis=1),
                                jnp.concatenate([z, xt[D:]], axis=1)], axis=0)

    def per_head(fn, x):
        return jnp.concatenate([fn(x[:, :D]), fn(x[:, D:])], axis=1)

    def unit(sc, gp):
        rs = slice(sc * S, (sc + 1) * S)
        sl = slice(2 * gp * D, (2 * gp + 2) * D)
        a = a_all[rs, sl]
        kk = kk_all[rs, sl]
        qq = qq_all[rs, sl]
        vb = v_all[rs, sl]

        qk = qq * kk
        diag = jnp.where(first, jnp.sum(qk[:, :D], axis=-1, keepdims=True),
                         jnp.sum(qk[:, D:], axis=-1, keepdims=True))
        att = jnp.where(row == col, diag, 0.0)
        for h, up, pr in zip(levels, upper, pair):
            fac = jnp.exp2(_neg_abs(a - _block_ref_rows(a, h)))
            pm = (jnp.where(up, qq, kk) * fac).astype(BF16)
            att = jnp.where(pr, _dot(pm, block_diag_cols(pm.T)), att)

        st = jnp.concatenate([st_ref[2 * gp], st_ref[2 * gp + 1]], axis=1)
        out = (_dot(att.astype(BF16), block_diag(vb))
               + _dot_nt((qq * jnp.exp2(a)).astype(BF16), block_diag(st.astype(BF16))))
        a_end = a[S - 1:S, :]
        upd = _dot_tn(vb, (kk * jnp.exp2(a_end - a)).astype(BF16))
        st_new = jnp.exp2(a_end) * st + jnp.concatenate([upd[:D, :D], upd[D:, D:]], axis=1)
        st_ref[2 * gp] = st_new[:, :D]
        st_ref[2 * gp + 1] = st_new[:, D:]

        ms = per_head(lambda o: jnp.broadcast_to(jnp.mean(o * o, axis=-1, keepdims=True), o.shape), out)
        hb_ref[rs, sl] = (out * lax.rsqrt(ms + HEAD_NORM_EPS) * og_all[rs, sl]).astype(hb_ref.dtype)

    return unit, [(sc, gp) for sc in range(L // S) for gp in range(H_HEADS // 2)]


def _mixer_kernel(layer, n_chunks, xcur_ref, xnext_ref, wb_ref, bb_ref, wf_ref, bf_ref, wgt_ref, bgt_ref,
                  cw_ref, cb_ref, mng_ref, lbl_ref, hng_ref, ha_ref, hb_ref,
                  ub_ref, ug_ref, gr_ref, uh_ref, uhf_ref, c_ref, n_ref, m_ref, tail_ref, st_ref):
    L = xcur_ref.shape[0]
    nv = L // SUBLANES
    k = pl.program_id(0)
    rr = lax.broadcasted_iota(jnp.int32, (L, L), 0)
    cc = lax.broadcasted_iota(jnp.int32, (L, L), 1)
    t_r = (rr % SUBLANES) * nv + rr // SUBLANES
    t_c = (cc % SUBLANES) * nv + cc // SUBLANES
    to_permuted = lambda x: _dot((cc == t_r).astype(BF16), x).astype(BF16)

    def add_rows(acc, b8):
        n = acc.shape[1]
        return (acc.reshape(L // SUBLANES, SUBLANES, n) + b8[None]).reshape(L, n)

    def gate_pieces(x_perm):
        def gates():
            ug_ref[...] = add_rows(_dot(x_perm, wf_ref[...]), bf_ref[...])
            gr_ref[...] = _dot_nt(wgt_ref[...], x_perm) + bgt_ref[...]
        return [gates]

    def head_pieces(x_perm, h):
        def piece(c):
            cs = slice(c, c + PIECE_COLS)
            ub_ref[:, cs] = add_rows(_dot(x_perm, wb_ref[:, cs]), bb_ref[:, cs]).astype(BF16)
        return [functools.partial(piece, c)
                for c in range(h * HEAD_COLS, (h + 1) * HEAD_COLS, PIECE_COLS)]

    def hgrn_pieces(x_nat):
        def piece(c):
            cs = slice(COL_HQ + c, COL_HQ + c + PIECE_COLS)
            acc = add_rows(_dot(x_nat, wb_ref[:, cs]), bb_ref[:, cs])
            if c < 3 * H_WIDTH:
                uh_ref[:, c:c + PIECE_COLS] = acc.astype(BF16)
            else:
                uhf_ref[:, c - 3 * H_WIDTH:c - 3 * H_WIDTH + PIECE_COLS] = acc
        return [functools.partial(piece, c) for c in range(0, 4 * H_WIDTH, PIECE_COLS)]

    def filler(pieces):
        it = iter(pieces)

        def fill(drain=False):
            for thunk in it:
                thunk()
                if not drain:
                    break
        return fill

    @pl.when(k == 0)
    def _():
        x0 = xcur_ref[...]
        xp0 = to_permuted(x0)
        for thunk in (gate_pieces(xp0) + [p for h in range(M_HEADS) for p in head_pieces(xp0, h)]
                      + hgrn_pieces(x0)):
            thunk()

    @pl.when(lax.rem(k, n_chunks) == 0)
    def _():
        c_ref[...] = jnp.zeros_like(c_ref)
        n_ref[...] = jnp.zeros_like(n_ref)
        m_ref[...] = jnp.zeros_like(m_ref)
        tail_ref[...] = jnp.zeros_like(tail_ref)
        st_ref[...] = jnp.zeros_like(st_ref)

    xn = xnext_ref[...]
    xpn = to_permuted(xn)
    gcol = ug_ref[...]
    grow = gr_ref[...]
    filler(gate_pieces(xpn))(drain=True)
    causal = t_c <= t_r
    tri = causal.astype(BF16)
    tri_t = (t_r <= t_c).astype(BF16)
    lf_hi, lf_lo = _split_bf16(_log_sigmoid(gcol))
    b_col = _dot(tri, lf_hi) + _dot(tri, lf_lo)
    lr_hi, lr_lo = _split_bf16(_log_sigmoid(grow))
    b_row = _dot(lr_hi, tri_t) + _dot(lr_lo, tri_t)

    hgrn_unit, units = _hgrn_chunk(layer, uh_ref, uhf_ref, lbl_ref, hng_ref, st_ref, hb_ref)
    units_per_head = -(-len(units) // M_HEADS)
    next_hgrn = hgrn_pieces(xn)
    hgrn_per_head = -(-len(next_hgrn) // M_HEADS)

    def head_filler(h):
        mine = head_pieces(xpn, h)
        ride = next_hgrn[h * hgrn_per_head:(h + 1) * hgrn_per_head]
        mixed = [p for pair in zip(mine, ride) for p in pair] + mine[len(ride):] + ride[len(mine):]
        return filler(mixed)

    def after_head(h, fill):
        for sc, g in units[h * units_per_head:(h + 1) * units_per_head]:
            fill()
            fill()
            hgrn_unit(sc, g)
        fill(drain=True)

    h_perm = _mlstm_chunk(ub_ref, causal, gcol, grow, b_col, b_row, cw_ref, cb_ref, mng_ref,
                          c_ref, n_ref, m_ref, tail_ref, head_filler, after_head)
    ha_ref[...] = _dot((rr == t_c).astype(BF16), h_perm).astype(ha_ref.dtype)


def _mixer(layer, xb3, wb, bb, wf, bf, wgt, bgt, conv_w, conv_b, m_norm_g, lb_logits, h_norm_g):
    bsz, s, d = xb3.shape
    L = min(MIX_CHUNK, s)
    n_chunks = s // L
    steps = bsz * n_chunks
    xk = xb3.reshape(steps, L, d)
    resident = lambda a: pl.BlockSpec(a.shape, lambda k: (0,) * a.ndim, pipeline_mode=pl.Buffered(1))
    rows = lambda w: pl.BlockSpec((None, L, w), lambda k: (k, 0, 0))
    next_rows = pl.BlockSpec((None, L, d), lambda k: (jnp.minimum(k + 1, steps - 1), 0, 0))
    consts = (wb, bb, wf, bf, wgt, bgt, conv_w, conv_b, m_norm_g, lb_logits, h_norm_g)
    return pl.pallas_call(
        functools.partial(_mixer_kernel, layer, n_chunks),
        grid=(steps,),
        in_specs=[rows(d), next_rows] + [resident(a) for a in consts],
        out_specs=[rows(M_WIDTH), rows(H_WIDTH)],
        out_shape=[jax.ShapeDtypeStruct((steps, L, M_WIDTH), BF16),
                   jax.ShapeDtypeStruct((steps, L, H_WIDTH), BF16)],
        scratch_shapes=[pltpu.VMEM((L, COL_HQ), BF16),
                        pltpu.VMEM((L, LANES), F32),
                        pltpu.VMEM((SUBLANES, L), F32),
                        pltpu.VMEM((L, 3 * H_WIDTH), BF16),
                        pltpu.VMEM((L, H_WIDTH), F32),
                        pltpu.VMEM((M_HEADS, M_HEAD_DIM, M_HEAD_DIM), F32),
                        pltpu.VMEM((M_HEADS, 1, M_HEAD_DIM), F32),
                        pltpu.VMEM((SUBLANES, LANES), F32),
                        pltpu.VMEM(((CONV_WIDTH - 1) * SUBLANES, 2 * M_WIDTH), F32),
                        pltpu.VMEM((H_HEADS, H_DIM, H_DIM), F32)],
        compiler_params=_params("arbitrary"),
        name="mixer",
    )(xk, xk, *consts)


def _merge_kernel(alpha, ha_ref, hb_ref, x_ref, xb_ref, wgab_ref, bgab_ref, pa_ref, pb_ref, wo_ref,
                  g_ref, b_ref, o_ref, ob_ref):
    gates = _sigmoid(_dot(xb_ref[...], wgab_ref[...]) + bgab_ref[...])
    d = x_ref.shape[1]
    ya = _dot(ha_ref[...], pa_ref[...])
    yb = _dot(hb_ref[...], pb_ref[...])
    mixed = gates[:, :d] * ya + gates[:, d:] * yb
    y = alpha * x_ref[...] + _dot(mixed.astype(BF16), wo_ref[...])
    out = _layernorm_rows(y, g_ref[...], b_ref[...])
    o_ref[...] = out
    ob_ref[...] = out.astype(BF16)


def _merge(alpha, ha, hb, x, xb, wgab, bgab, pa, pb, wo, g, b):
    t, d = x.shape
    tm = min(MERGE_TM, t)
    rows = lambda w: pl.BlockSpec((tm, w), lambda i: (i, 0))
    full = lambda a: pl.BlockSpec(a.shape, lambda i: (0,) * a.ndim, pipeline_mode=pl.Buffered(1))
    consts = (wgab, bgab, pa, pb, wo, g, b)
    return pl.pallas_call(
        functools.partial(_merge_kernel, alpha),
        grid=(t // tm,),
        in_specs=[rows(M_WIDTH), rows(H_WIDTH), rows(d), rows(d)] + [full(a) for a in consts],
        out_specs=[rows(d), rows(d)],
        out_shape=[jax.ShapeDtypeStruct((t, d), F32), jax.ShapeDtypeStruct((t, d), BF16)],
        compiler_params=_params("parallel"),
        name="merge_ln",
    )(ha, hb, x, xb, *consts)


def _ffn_kernel(alpha, x_ref, xb_ref, wg_ref, wu_ref, wd_ref, g_ref, b_ref, o_ref, ob_ref):
    xb = xb_ref[...]
    gate = _dot(xb, wg_ref[...])
    up = _dot(xb, wu_ref[...])
    hid = (_silu(gate) * up).astype(BF16)
    y = alpha * x_ref[...] + _dot(hid, wd_ref[...])
    out = _layernorm_rows(y, g_ref[...], b_ref[...])
    o_ref[...] = out
    ob_ref[...] = out.astype(BF16)


def _ffn(alpha, x, xb, wg, wu, wd, g, b):
    t, d = x.shape
    tm = min(FFN_TM, t)
    rows = pl.BlockSpec((tm, d), lambda i: (i, 0))
    full = lambda a: pl.BlockSpec(a.shape, lambda i: (0,) * a.ndim, pipeline_mode=pl.Buffered(1))
    return pl.pallas_call(
        functools.partial(_ffn_kernel, alpha),
        grid=(t // tm,),
        in_specs=[rows, rows, full(wg), full(wu), full(wd), full(g), full(b)],
        out_specs=[rows, rows],
        out_shape=[jax.ShapeDtypeStruct((t, d), F32), jax.ShapeDtypeStruct((t, d), BF16)],
        compiler_params=_params("parallel"),
        name="ffn_ln",
    )(x, xb, wg, wu, wd, g, b)


def _in_proj_layout(w_in, b_in):
    sizes = (M_WIDTH,) * 4 + (M_HEADS, M_HEADS) + (H_WIDTH,) * 6
    off = [int(o) for o in np.cumsum((0,) + sizes)]
    seg = lambda a, i: a[..., off[i]:off[i + 1]]
    rows8 = lambda b: jnp.broadcast_to(b[:, None, :], (b.shape[0], SUBLANES, b.shape[1]))
    dh = M_HEAD_DIM

    def main_group(a):
        heads = [seg(a, i)[..., h * dh:(h + 1) * dh] for h in range(M_HEADS) for i in (0, 1, 2, 3)]
        return jnp.concatenate(heads + [seg(a, i) for i in (6, 8, 9, 7)], axis=-1)

    wb = main_group(w_in).astype(BF16)
    bb = rows8(main_group(b_in))
    wgab = jnp.concatenate([seg(w_in, 10), seg(w_in, 11)], axis=-1).astype(BF16)
    bgab = jnp.concatenate([seg(b_in, 10), seg(b_in, 11)], axis=-1)[:, None, :]
    pad = LANES - 2 * M_HEADS
    wf = jnp.concatenate([seg(w_in, 4), seg(w_in, 5),
                          jnp.zeros(w_in.shape[:2] + (pad,), w_in.dtype)], axis=-1).astype(BF16)
    bf = rows8(jnp.concatenate([seg(b_in, 4), seg(b_in, 5),
                                jnp.zeros(b_in.shape[:1] + (pad,), b_in.dtype)], axis=-1))
    wgt = jnp.swapaxes(jnp.concatenate([seg(w_in, 4), seg(w_in, 5)], axis=-1), 1, 2).astype(BF16)
    bgt = jnp.concatenate([seg(b_in, 4), seg(b_in, 5)], axis=-1)[:, :, None]
    return wb, bb, wf, bf, wgt, bgt, wgab, bgab


def kernel(x, w_in, b_in, conv_w, conv_b, m_norm_g, lb_logits, h_norm_g, w_proj_a, w_proj_b, w_out,
           ln1_g, ln1_b, w_ffn_gate, w_ffn_up, w_ffn_down, ln2_g, ln2_b):
    bsz, s, d = x.shape
    depth = w_in.shape[0]
    alpha = float((2 * depth) ** 0.25)
    t = bsz * s

    wb, bb, wf, bf, wgt, bgt, wgab, bgab = _in_proj_layout(w_in, b_in)
    pa = w_proj_a.astype(BF16)
    pb = w_proj_b.astype(BF16)
    wo = w_out.astype(BF16)
    wg = w_ffn_gate.astype(BF16)
    wu = w_ffn_up.astype(BF16)
    wd = w_ffn_down.astype(BF16)

    xf = x.reshape(t, d)
    xb = xf.astype(BF16)
    for layer in range(depth):
        ha, hb = _mixer(layer, xb.reshape(bsz, s, d), wb[layer], bb[layer], wf[layer], bf[layer],
                        wgt[layer], bgt[layer], conv_w[layer], conv_b[layer][None, :],
                        m_norm_g[layer][None, :], lb_logits, h_norm_g[layer][None, :])
        xf, xb = _merge(alpha, ha.reshape(t, M_WIDTH), hb.reshape(t, H_WIDTH), xf, xb,
                        wgab[layer], bgab[layer], pa[layer], pb[layer], wo[layer],
                        ln1_g[layer][None, :], ln1_b[layer][None, :])
        xf, xb = _ffn(alpha, xf, xb, wg[layer], wu[layer], wd[layer],
                      ln2_g[layer][None, :], ln2_b[layer][None, :])
    return xf.reshape(bsz, s, d)
```

```python
import functools

import numpy as np
import jax
import jax.numpy as jnp
from jax import lax
from jax.experimental import pallas as pl
from jax.experimental.pallas import tpu as pltpu

F32 = jnp.float32
BF16 = jnp.bfloat16

M_HEADS = 4
M_HEAD_DIM = 512
M_WIDTH = M_HEADS * M_HEAD_DIM
CONV_WIDTH = 4
H_HEADS = 8
H_DIM = 128
H_WIDTH = H_HEADS * H_DIM
LN_EPS = 1e-5
HEAD_NORM_EPS = 1e-6

SUBLANES = 8
LANES = 128
VMEM_LIMIT_BYTES = 60 * 1024 * 1024

MIX_CHUNK = 256
H_CHUNK = 128
MERGE_TM = 512
FFN_TM = 512

HEAD_COLS = 4 * M_HEAD_DIM
COL_HQ = M_HEADS * HEAD_COLS
PIECE_COLS = 256


NEG_LOG2_E = -1.4426950408889634


def _exp_neg(x):
    return jnp.exp2(x * NEG_LOG2_E)


def _neg_abs(x):
    bits = pltpu.bitcast(x, jnp.uint32) | jnp.uint32(0x80000000)
    return pltpu.bitcast(bits, F32)


def _sigmoid(x):
    return 1.0 / (1.0 + _exp_neg(x))


def _silu(x):
    return x * _sigmoid(x)


def _log_sigmoid(x):
    return jnp.minimum(x, 0.0) - jnp.log1p(_exp_neg(jnp.abs(x)))


def _split_bf16(x):
    hi = x.astype(BF16)
    lo = (x - hi.astype(F32)).astype(BF16)
    return hi, lo


def _dot(a, b):
    return jnp.dot(a, b, preferred_element_type=F32)


def _dot_nt(a, b):
    return lax.dot_general(a, b, (((1,), (1,)), ((), ())), preferred_element_type=F32)


def _dot_tn(a, b):
    return lax.dot_general(a, b, (((0,), (0,)), ((), ())), preferred_element_type=F32)


def _layernorm_rows(y, g, b):
    mu = jnp.mean(y, axis=-1, keepdims=True)
    d = y - mu
    var = jnp.mean(d * d, axis=-1, keepdims=True)
    return d * lax.rsqrt(var + LN_EPS) * g + b


def _params(*semantics):
    return pltpu.CompilerParams(dimension_semantics=semantics, vmem_limit_bytes=VMEM_LIMIT_BYTES)


def _conv_silu_permuted(pre, tail, w, bias):
    rows = pre.shape[0]
    sub = lax.broadcasted_iota(jnp.int32, (SUBLANES, pre.shape[1]), 0)
    acc = pre * w[CONV_WIDTH - 1:CONV_WIDTH, :] + bias
    for d in range(1, CONV_WIDTH):
        cur = pre[rows - d * SUBLANES:, :]
        prv = tail[(CONV_WIDTH - 1 - d) * SUBLANES:, :]
        tiles = []
        for i in range(d):
            ts = slice(i * SUBLANES, (i + 1) * SUBLANES)
            tiles.append(jnp.where(sub == 0, pltpu.roll(prv[ts], 1, 0), pltpu.roll(cur[ts], 1, 0)))
        shifted = jnp.concatenate(tiles + [pre[:rows - d * SUBLANES, :]], axis=0)
        acc = acc + shifted * w[CONV_WIDTH - 1 - d:CONV_WIDTH - d, :]
    return _silu(acc)


def _mlstm_chunk(ub_ref, causal, gcol, grow, b_col, b_row, cw_ref, cb_ref, ng_ref,
                 c_ref, n_ref, m_ref, tail_ref, head_filler, after_head):
    L = ub_ref.shape[0]
    dh = M_HEAD_DIM
    log_scale = float(np.log(dh ** -0.5))
    tail_rows = (CONV_WIDTH - 1) * SUBLANES
    outs = []
    for h in range(M_HEADS):
        sl = slice(h * dh, (h + 1) * dh)
        slk = slice(M_WIDTH + h * dh, M_WIDTH + (h + 1) * dh)
        c0 = h * HEAD_COLS
        q_pre = ub_ref[:, c0:c0 + dh].astype(F32)
        k_pre = ub_ref[:, c0 + dh:c0 + 2 * dh].astype(F32)
        v = ub_ref[:, c0 + 2 * dh:c0 + 3 * dh]
        o_gate = ub_ref[:, c0 + 3 * dh:c0 + 4 * dh]
        fill = head_filler(h)

        fill()
        q = _conv_silu_permuted(q_pre, tail_ref[:, sl], cw_ref[:, sl], cb_ref[:, sl])
        fill()
        k = _conv_silu_permuted(k_pre, tail_ref[:, slk], cw_ref[:, slk], cb_ref[:, slk])
        tail_ref[:, sl] = q_pre[L - tail_rows:, :]
        tail_ref[:, slk] = k_pre[L - tail_rows:, :]
        qb = q.astype(BF16)
        kb = k.astype(BF16)
        fill()

        m_prev = m_ref[h:h + 1, 0:1]
        r_row = grow[h:h + 1, :] - b_row[M_HEADS + h:M_HEADS + h + 1, :]
        bh_col = b_col[:, M_HEADS + h:M_HEADS + h + 1]
        r_col = gcol[:, h:h + 1] - bh_col
        big_m = jnp.maximum(jnp.max(jnp.where(causal, r_row, -jnp.inf), axis=-1, keepdims=True),
                            m_prev)
        decay_w = jnp.exp(jnp.where(causal, (r_row + log_scale) - big_m, -jnp.inf))
        s_inter = jnp.exp(m_prev - big_m)
        floor = _exp_neg(bh_col + big_m)

        w = decay_w * _dot_nt(qb, kb)
        fill()
        num = _dot(w.astype(BF16), v) + s_inter * _dot(qb, c_ref[h].astype(BF16))
        den = (jnp.sum(w, axis=-1, keepdims=True)
               + s_inter * jnp.sum(q * n_ref[h], axis=-1, keepdims=True))
        hh = num / jnp.maximum(jnp.abs(den), floor)
        fill()

        m_end = big_m[L - 1:L, :]
        decay = jnp.exp(m_prev - m_end)
        gk = jnp.exp((r_col + log_scale) - m_end) * k
        c_ref[h] = decay * c_ref[h] + _dot_tn(gk.astype(BF16), v)
        n_ref[h] = decay * n_ref[h] + jnp.sum(gk, axis=0, keepdims=True)
        m_ref[h:h + 1, :] = jnp.broadcast_to(bh_col[L - 1:L, :] + m_end, (1, LANES))
        fill()

        mu = jnp.mean(hh, axis=-1, keepdims=True)
        dlt = hh - mu
        var = jnp.mean(dlt * dlt, axis=-1, keepdims=True)
        hn = dlt * lax.rsqrt(var + HEAD_NORM_EPS) * ng_ref[:, sl]
        outs.append((hn * _sigmoid(o_gate.astype(F32))).astype(BF16))
        after_head(h, fill)
    return jnp.concatenate(outs, axis=-1)


def _block_ref_rows(a, h):
    L, n = a.shape
    if h >= SUBLANES:
        a4 = a.reshape(L // (2 * h), 2 * h, n)
        return jnp.broadcast_to(a4[:, h - 1:h, :], a4.shape).reshape(L, n)
    a3 = a.reshape(L // SUBLANES, SUBLANES, n)
    sub = lax.broadcasted_iota(jnp.int32, a3.shape, 1)
    if h == 4:
        ref = jnp.broadcast_to(a3[:, 3:4, :], a3.shape)
    elif h == 2:
        ref = jnp.where(sub < 4, jnp.broadcast_to(a3[:, 1:2, :], a3.shape),
                        jnp.broadcast_to(a3[:, 5:6, :], a3.shape))
    else:
        ref = jnp.where(sub % 2 == 1, pltpu.roll(a, 1, 0).reshape(a3.shape), a3)
    return ref.reshape(L, n)


def _hgrn_chunk(layer, uh_ref, uhf_ref, lbl_ref, ng_ref, st_ref, hb_ref):
    L = uh_ref.shape[0]
    S = min(H_CHUNK, L)
    levels = [1 << i for i in range(S.bit_length() - 1)]
    hq = uh_ref[:, :H_WIDTH].astype(F32)
    v_all = uh_ref[:, H_WIDTH:2 * H_WIDTH]
    hg = uh_ref[:, 2 * H_WIDTH:].astype(F32)
    hf = uhf_ref[...]

    logits = lbl_ref[...]
    p = jnp.exp(logits - jnp.max(logits, axis=0, keepdims=True))
    p = p / jnp.sum(p, axis=0, keepdims=True)
    lb = jnp.zeros((1, H_WIDTH), F32)
    for d in range(1, layer + 1):
        lb = lb + p[d:d + 1, :]
    softplus_tail = jnp.log(1.0 + _exp_neg(jnp.abs(hf)))
    log1m_lb = jnp.log1p(-lb)
    x2 = log1m_lb + jnp.minimum(hf, 0.0) - softplus_tail
    if layer == 0:
        lf = x2
    else:
        x1 = jnp.log(lb)
        lf = jnp.maximum(x1, x2) + jnp.log(1.0 + _exp_neg(jnp.abs(x1 - x2)))
    kk_all = jnp.exp(log1m_lb - jnp.maximum(hf, 0.0) - softplus_tail)
    qq_all = _silu(hq)
    og_all = _sigmoid(hg) * ng_ref[...]

    rl = lax.broadcasted_iota(jnp.int32, (L, L), 0)
    cl = lax.broadcasted_iota(jnp.int32, (L, L), 1)
    tri = ((cl <= rl) & (rl // S == cl // S)).astype(BF16)
    lf_hi, lf_lo = _split_bf16(lf * (-NEG_LOG2_E))
    a_all = _dot(tri, lf_hi) + _dot(tri, lf_lo)

    D = H_DIM
    row = lax.broadcasted_iota(jnp.int32, (S, 2 * S), 0)
    col = lax.broadcasted_iota(jnp.int32, (S, 2 * S), 1) % S
    trow = lax.broadcasted_iota(jnp.int32, (S, 1), 0)
    upper = [(trow // h) % 2 == 1 for h in levels]
    pair = [(row // (2 * h) == col // (2 * h)) & ((row // h) % 2 == 1) & ((col // h) % 2 == 0)
            for h in levels]
    first = lax.broadcasted_iota(jnp.int32, (S, 2 * S), 1) < S

    def block_diag(x):
        z = jnp.zeros((x.shape[0], D), x.dtype)
        return jnp.concatenate([jnp.concatenate([x[:, :D], z], axis=1),
                                jnp.concatenate([z, x[:, D:]], axis=1)], axis=0)

    def block_diag_cols(xt):
        z = jnp.zeros((D, xt.shape[1]), xt.dtype)
        return jnp.concatenate([jnp.concatenate([xt[:D], z], axis=1),
                                jnp.concatenate([z, xt[D:]], axis=1)], axis=0)

    def per_head(fn, x):
        return jnp.concatenate([fn(x[:, :D]), fn(x[:, D:])], axis=1)

    def unit(sc, gp):
        rs = slice(sc * S, (sc + 1) * S)
        sl = slice(2 * gp * D, (2 * gp + 2) * D)
        a = a_all[rs, sl]
        kk = kk_all[rs, sl]
        qq = qq_all[rs, sl]
        vb = v_all[rs, sl]

        qk = qq * kk
        diag = jnp.where(first, jnp.sum(qk[:, :D], axis=-1, keepdims=True),
                         jnp.sum(qk[:, D:], axis=-1, keepdims=True))
        att = jnp.where(row == col, diag, 0.0)
        for h, up, pr in zip(levels, upper, pair):
            fac = jnp.exp2(_neg_abs(a - _block_ref_rows(a, h)))
            pm = (jnp.where(up, qq, kk) * fac).astype(BF16)
            att = jnp.where(pr, _dot(pm, block_diag_cols(pm.T)), att)

        st = jnp.concatenate([st_ref[2 * gp], st_ref[2 * gp + 1]], axis=1)
        out = (_dot(att.astype(BF16), block_diag(vb))
               + _dot_nt((qq * jnp.exp2(a)).astype(BF16), block_diag(st.astype(BF16))))
        a_end = a[S - 1:S, :]
        upd = _dot_tn(vb, (kk * jnp.exp2(a_end - a)).astype(BF16))
        st_new = jnp.exp2(a_end) * st + jnp.concatenate([upd[:D, :D], upd[D:, D:]], axis=1)
        st_ref[2 * gp] = st_new[:, :D]
        st_ref[2 * gp + 1] = st_new[:, D:]

        ms = per_head(lambda o: jnp.broadcast_to(jnp.mean(o * o, axis=-1, keepdims=True), o.shape), out)
        hb_ref[rs, sl] = (out * lax.rsqrt(ms + HEAD_NORM_EPS) * og_all[rs, sl]).astype(hb_ref.dtype)

    return unit, [(sc, gp) for sc in range(L // S) for gp in range(H_HEADS // 2)]


def _mixer_kernel(layer, n_chunks, xcur_ref, xnext_ref, wb_ref, bb_ref, wf_ref, bf_ref, wgt_ref, bgt_ref,
                  cw_ref, cb_ref, mng_ref, lbl_ref, hng_ref, ha_ref, hb_ref,
                  ub_ref, ug_ref, gr_ref, uh_ref, uhf_ref, c_ref, n_ref, m_ref, tail_ref, st_ref):
    L = xcur_ref.shape[0]
    nv = L // SUBLANES
    k = pl.program_id(0)
    rr = lax.broadcasted_iota(jnp.int32, (L, L), 0)
    cc = lax.broadcasted_iota(jnp.int32, (L, L), 1)
    t_r = (rr % SUBLANES) * nv + rr // SUBLANES
    t_c = (cc % SUBLANES) * nv + cc // SUBLANES
    to_permuted = lambda x: _dot((cc == t_r).astype(BF16), x).astype(BF16)

    def add_rows(acc, b8):
        n = acc.shape[1]
        return (acc.reshape(L // SUBLANES, SUBLANES, n) + b8[None]).reshape(L, n)

    def gate_pieces(x_perm):
        def gates():
            ug_ref[...] = add_rows(_dot(x_perm, wf_ref[...]), bf_ref[...])
            gr_ref[...] = _dot_nt(wgt_ref[...], x_perm) + bgt_ref[...]
        return [gates]

    def head_pieces(x_perm, h):
        def piece(c):
            cs = slice(c, c + PIECE_COLS)
            ub_ref[:, cs] = add_rows(_dot(x_perm, wb_ref[:, cs]), bb_ref[:, cs]).astype(BF16)
        return [functools.partial(piece, c)
                for c in range(h * HEAD_COLS, (h + 1) * HEAD_COLS, PIECE_COLS)]

    def hgrn_pieces(x_nat):
        def piece(c):
            cs = slice(COL_HQ + c, COL_HQ + c + PIECE_COLS)
            acc = add_rows(_dot(x_nat, wb_ref[:, cs]), bb_ref[:, cs])
            if c < 3 * H_WIDTH:
                uh_ref[:, c:c + PIECE_COLS] = acc.astype(BF16)
            else:
                uhf_ref[:, c - 3 * H_WIDTH:c - 3 * H_WIDTH + PIECE_COLS] = acc
        return [functools.partial(piece, c) for c in range(0, 4 * H_WIDTH, PIECE_COLS)]

    def filler(pieces):
        it = iter(pieces)

        def fill(drain=False):
            for thunk in it:
                thunk()
                if not drain:
                    break
        return fill

    @pl.when(k == 0)
    def _():
        x0 = xcur_ref[...]
        xp0 = to_permuted(x0)
        for thunk in (gate_pieces(xp0) + [p for h in range(M_HEADS) for p in head_pieces(xp0, h)]
                      + hgrn_pieces(x0)):
            thunk()

    @pl.when(lax.rem(k, n_chunks) == 0)
    def _():
        c_ref[...] = jnp.zeros_like(c_ref)
        n_ref[...] = jnp.zeros_like(n_ref)
        m_ref[...] = jnp.zeros_like(m_ref)
        tail_ref[...] = jnp.zeros_like(tail_ref)
        st_ref[...] = jnp.zeros_like(st_ref)

    xn = xnext_ref[...]
    xpn = to_permuted(xn)
    gcol = ug_ref[...]
    grow = gr_ref[...]
    filler(gate_pieces(xpn))(drain=True)
    causal = t_c <= t_r
    tri = causal.astype(BF16)
    tri_t = (t_r <= t_c).astype(BF16)
    lf_hi, lf_lo = _split_bf16(_log_sigmoid(gcol))
    b_col = _dot(tri, lf_hi) + _dot(tri, lf_lo)
    lr_hi, lr_lo = _split_bf16(_log_sigmoid(grow))
    b_row = _dot(lr_hi, tri_t) + _dot(lr_lo, tri_t)

    hgrn_unit, units = _hgrn_chunk(layer, uh_ref, uhf_ref, lbl_ref, hng_ref, st_ref, hb_ref)
    units_per_head = -(-len(units) // M_HEADS)
    next_hgrn = hgrn_pieces(xn)
    hgrn_per_head = -(-len(next_hgrn) // M_HEADS)

    def head_filler(h):
        mine = head_pieces(xpn, h)
        ride = next_hgrn[h * hgrn_per_head:(h + 1) * hgrn_per_head]
        mixed = [p for pair in zip(mine, ride) for p in pair] + mine[len(ride):] + ride[len(mine):]
        return filler(mixed)

    def after_head(h, fill):
        for sc, g in units[h * units_per_head:(h + 1) * units_per_head]:
            fill()
            fill()
            hgrn_unit(sc, g)
        fill(drain=True)

    h_perm = _mlstm_chunk(ub_ref, causal, gcol, grow, b_col, b_row, cw_ref, cb_ref, mng_ref,
                          c_ref, n_ref, m_ref, tail_ref, head_filler, after_head)
    ha_ref[...] = _dot((rr == t_c).astype(BF16), h_perm).astype(ha_ref.dtype)


def _mixer(layer, xb3, wb, bb, wf, bf, wgt, bgt, conv_w, conv_b, m_norm_g, lb_logits, h_norm_g):
    bsz, s, d = xb3.shape
    L = min(MIX_CHUNK, s)
    assert s % L == 0 and L % min(H_CHUNK, L) == 0 and L % (2 * SUBLANES) == 0, (s, L)
    n_chunks = s // L
    steps = bsz * n_chunks
    xk = xb3.reshape(steps, L, d)
    resident = lambda a: pl.BlockSpec(a.shape, lambda k: (0,) * a.ndim, pipeline_mode=pl.Buffered(1))
    rows = lambda w: pl.BlockSpec((None, L, w), lambda k: (k, 0, 0))
    next_rows = pl.BlockSpec((None, L, d), lambda k: (jnp.minimum(k + 1, steps - 1), 0, 0))
    consts = (wb, bb, wf, bf, wgt, bgt, conv_w, conv_b, m_norm_g, lb_logits, h_norm_g)
    return pl.pallas_call(
        functools.partial(_mixer_kernel, layer, n_chunks),
        grid=(steps,),
        in_specs=[rows(d), next_rows] + [resident(a) for a in consts],
        out_specs=[rows(M_WIDTH), rows(H_WIDTH)],
        out_shape=[jax.ShapeDtypeStruct((steps, L, M_WIDTH), BF16),
                   jax.ShapeDtypeStruct((steps, L, H_WIDTH), BF16)],
        scratch_shapes=[pltpu.VMEM((L, COL_HQ), BF16),
                        pltpu.VMEM((L, LANES), F32),
                        pltpu.VMEM((SUBLANES, L), F32),
                        pltpu.VMEM((L, 3 * H_WIDTH), BF16),
                        pltpu.VMEM((L, H_WIDTH), F32),
                        pltpu.VMEM((M_HEADS, M_HEAD_DIM, M_HEAD_DIM), F32),
                        pltpu.VMEM((M_HEADS, 1, M_HEAD_DIM), F32),
                        pltpu.VMEM((SUBLANES, LANES), F32),
                        pltpu.VMEM(((CONV_WIDTH - 1) * SUBLANES, 2 * M_WIDTH), F32),
                        pltpu.VMEM((H_HEADS, H_DIM, H_DIM), F32)],
        compiler_params=_params("arbitrary"),
        name="mixer",
    )(xk, xk, *consts)


def _merge_kernel(alpha, ha_ref, hb_ref, x_ref, xb_ref, wgab_ref, bgab_ref, pa_ref, pb_ref, wo_ref,
                  g_ref, b_ref, o_ref, ob_ref):
    gates = _sigmoid(_dot(xb_ref[...], wgab_ref[...]) + bgab_ref[...])
    d = x_ref.shape[1]
    ya = _dot(ha_ref[...], pa_ref[...])
    yb = _dot(hb_ref[...], pb_ref[...])
    mixed = gates[:, :d] * ya + gates[:, d:] * yb
    y = alpha * x_ref[...] + _dot(mixed.astype(BF16), wo_ref[...])
    out = _layernorm_rows(y, g_ref[...], b_ref[...])
    o_ref[...] = out
    ob_ref[...] = out.astype(BF16)


def _merge(alpha, ha, hb, x, xb, wgab, bgab, pa, pb, wo, g, b):
    t, d = x.shape
    tm = min(MERGE_TM, t)
    assert t % tm == 0, (t, tm)
    rows = lambda w: pl.BlockSpec((tm, w), lambda i: (i, 0))
    full = lambda a: pl.BlockSpec(a.shape, lambda i: (0,) * a.ndim, pipeline_mode=pl.Buffered(1))
    consts = (wgab, bgab, pa, pb, wo, g, b)
    return pl.pallas_call(
        functools.partial(_merge_kernel, alpha),
        grid=(t // tm,),
        in_specs=[rows(M_WIDTH), rows(H_WIDTH), rows(d), rows(d)] + [full(a) for a in consts],
        out_specs=[rows(d), rows(d)],
        out_shape=[jax.ShapeDtypeStruct((t, d), F32), jax.ShapeDtypeStruct((t, d), BF16)],
        compiler_params=_params("parallel"),
        name="merge_ln",
    )(ha, hb, x, xb, *consts)


def _ffn_kernel(alpha, x_ref, xb_ref, wg_ref, wu_ref, wd_ref, g_ref, b_ref, o_ref, ob_ref):
    xb = xb_ref[...]
    gate = _dot(xb, wg_ref[...])
    up = _dot(xb, wu_ref[...])
    hid = (_silu(gate) * up).astype(BF16)
    y = alpha * x_ref[...] + _dot(hid, wd_ref[...])
    out = _layernorm_rows(y, g_ref[...], b_ref[...])
    o_ref[...] = out
    ob_ref[...] = out.astype(BF16)


def _ffn(alpha, x, xb, wg, wu, wd, g, b):
    t, d = x.shape
    tm = min(FFN_TM, t)
    assert t % tm == 0, (t, tm)
    rows = pl.BlockSpec((tm, d), lambda i: (i, 0))
    full = lambda a: pl.BlockSpec(a.shape, lambda i: (0,) * a.ndim, pipeline_mode=pl.Buffered(1))
    return pl.pallas_call(
        functools.partial(_ffn_kernel, alpha),
        grid=(t // tm,),
        in_specs=[rows, rows, full(wg), full(wu), full(wd), full(g), full(b)],
        out_specs=[rows, rows],
        out_shape=[jax.ShapeDtypeStruct((t, d), F32), jax.ShapeDtypeStruct((t, d), BF16)],
        compiler_params=_params("parallel"),
        name="ffn_ln",
    )(x, xb, wg, wu, wd, g, b)


def _in_proj_layout(w_in, b_in):
    sizes = (M_WIDTH,) * 4 + (M_HEADS, M_HEADS) + (H_WIDTH,) * 6
    off = [int(o) for o in np.cumsum((0,) + sizes)]
    seg = lambda a, i: a[..., off[i]:off[i + 1]]
    rows8 = lambda b: jnp.broadcast_to(b[:, None, :], (b.shape[0], SUBLANES, b.shape[1]))
    dh = M_HEAD_DIM

    def main_group(a):
        heads = [seg(a, i)[..., h * dh:(h + 1) * dh] for h in range(M_HEADS) for i in (0, 1, 2, 3)]
        return jnp.concatenate(heads + [seg(a, i) for i in (6, 8, 9, 7)], axis=-1)

    wb = main_group(w_in).astype(BF16)
    bb = rows8(main_group(b_in))
    wgab = jnp.concatenate([seg(w_in, 10), seg(w_in, 11)], axis=-1).astype(BF16)
    bgab = jnp.concatenate([seg(b_in, 10), seg(b_in, 11)], axis=-1)[:, None, :]
    pad = LANES - 2 * M_HEADS
    wf = jnp.concatenate([seg(w_in, 4), seg(w_in, 5),
                          jnp.zeros(w_in.shape[:2] + (pad,), w_in.dtype)], axis=-1).astype(BF16)
    bf = rows8(jnp.concatenate([seg(b_in, 4), seg(b_in, 5),
                                jnp.zeros(b_in.shape[:1] + (pad,), b_in.dtype)], axis=-1))
    wgt = jnp.swapaxes(jnp.concatenate([seg(w_in, 4), seg(w_in, 5)], axis=-1), 1, 2).astype(BF16)
    bgt = jnp.concatenate([seg(b_in, 4), seg(b_in, 5)], axis=-1)[:, :, None]
    return wb, bb, wf, bf, wgt, bgt, wgab, bgab


def kernel(x, w_in, b_in, conv_w, conv_b, m_norm_g, lb_logits, h_norm_g, w_proj_a, w_proj_b, w_out,
           ln1_g, ln1_b, w_ffn_gate, w_ffn_up, w_ffn_down, ln2_g, ln2_b):
    bsz, s, d = x.shape
    depth = w_in.shape[0]
    alpha = float((2 * depth) ** 0.25)
    t = bsz * s

    wb, bb, wf, bf, wgt, bgt, wgab, bgab = _in_proj_layout(w_in, b_in)
    pa = w_proj_a.astype(BF16)
    pb = w_proj_b.astype(BF16)
    wo = w_out.astype(BF16)
    wg = w_ffn_gate.astype(BF16)
    wu = w_ffn_up.astype(BF16)
    wd = w_ffn_down.astype(BF16)

    xf = x.reshape(t, d)
    xb = xf.astype(BF16)
    for layer in range(depth):
        ha, hb = _mixer(layer, xb.reshape(bsz, s, d), wb[layer], bb[layer], wf[layer], bf[layer],
                        wgt[layer], bgt[layer], conv_w[layer], conv_b[layer][None, :],
                        m_norm_g[layer][None, :], lb_logits, h_norm_g[layer][None, :])
        xf, xb = _merge(alpha, ha.reshape(t, M_WIDTH), hb.reshape(t, H_WIDTH), xf, xb,
                        wgab[layer], bgab[layer], pa[layer], pb[layer], wo[layer],
                        ln1_g[layer][None, :], ln1_b[layer][None, :])
        xf, xb = _ffn(alpha, xf, xb, wg[layer], wu[layer], wd[layer],
                      ln2_g[layer][None, :], ln2_b[layer][None, :])
    return xf.reshape(bsz, s, d)
```

```python
import functools

import numpy as np
import jax
import jax.numpy as jnp
from jax import lax
from jax.experimental import pallas as pl
from jax.experimental.pallas import tpu as pltpu

F32 = jnp.float32
BF16 = jnp.bfloat16

M_HEADS = 4
M_HEAD_DIM = 512
M_WIDTH = M_HEADS * M_HEAD_DIM
CONV_WIDTH = 4
H_HEADS = 8
H_DIM = 128
H_WIDTH = H_HEADS * H_DIM
LN_EPS = 1e-5
HEAD_NORM_EPS = 1e-6

SUBLANES = 8
LANES = 128
VMEM_LIMIT_BYTES = 60 * 1024 * 1024

MIX_CHUNK = 256
H_CHUNK = 128
TAIL_TM = 256

HEAD_COLS = 4 * M_HEAD_DIM
COL_HQ = M_HEADS * HEAD_COLS
PIECE_COLS = 256


NEG_LOG2_E = -1.4426950408889634


def _exp_neg(x):
    return jnp.exp2(x * NEG_LOG2_E)


def _neg_abs(x):
    bits = pltpu.bitcast(x, jnp.uint32) | jnp.uint32(0x80000000)
    return pltpu.bitcast(bits, F32)


def _sigmoid(x):
    return 1.0 / (1.0 + _exp_neg(x))


def _silu(x):
    return x * _sigmoid(x)


def _log_sigmoid(x):
    return jnp.minimum(x, 0.0) - jnp.log1p(_exp_neg(jnp.abs(x)))


def _split_bf16(x):
    hi = x.astype(BF16)
    lo = (x - hi.astype(F32)).astype(BF16)
    return hi, lo


def _dot(a, b):
    return jnp.dot(a, b, preferred_element_type=F32)


def _dot_nt(a, b):
    return lax.dot_general(a, b, (((1,), (1,)), ((), ())), preferred_element_type=F32)


def _dot_tn(a, b):
    return lax.dot_general(a, b, (((0,), (0,)), ((), ())), preferred_element_type=F32)


def _layernorm_rows(y, g, b):
    mu = jnp.mean(y, axis=-1, keepdims=True)
    d = y - mu
    var = jnp.mean(d * d, axis=-1, keepdims=True)
    return d * lax.rsqrt(var + LN_EPS) * g + b


def _params(*semantics):
    return pltpu.CompilerParams(dimension_semantics=semantics, vmem_limit_bytes=VMEM_LIMIT_BYTES)


def _conv_silu_permuted(pre, tail, w, bias):
    rows = pre.shape[0]
    sub = lax.broadcasted_iota(jnp.int32, (SUBLANES, pre.shape[1]), 0)
    acc = pre * w[CONV_WIDTH - 1:CONV_WIDTH, :] + bias
    for d in range(1, CONV_WIDTH):
        cur = pre[rows - d * SUBLANES:, :]
        prv = tail[(CONV_WIDTH - 1 - d) * SUBLANES:, :]
        tiles = []
        for i in range(d):
            ts = slice(i * SUBLANES, (i + 1) * SUBLANES)
            tiles.append(jnp.where(sub == 0, pltpu.roll(prv[ts], 1, 0), pltpu.roll(cur[ts], 1, 0)))
        shifted = jnp.concatenate(tiles + [pre[:rows - d * SUBLANES, :]], axis=0)
        acc = acc + shifted * w[CONV_WIDTH - 1 - d:CONV_WIDTH - d, :]
    return _silu(acc)


def _mlstm_chunk(ub_ref, causal, gcol, grow, b_col, b_row, cw_ref, cb_ref, ng_ref,
                 c_ref, n_ref, m_ref, tail_ref, head_filler, after_head):
    L = ub_ref.shape[0]
    dh = M_HEAD_DIM
    log_scale = float(np.log(dh ** -0.5))
    tail_rows = (CONV_WIDTH - 1) * SUBLANES
    outs = []
    for h in range(M_HEADS):
        sl = slice(h * dh, (h + 1) * dh)
        slk = slice(M_WIDTH + h * dh, M_WIDTH + (h + 1) * dh)
        c0 = h * HEAD_COLS
        q_pre = ub_ref[:, c0:c0 + dh].astype(F32)
        k_pre = ub_ref[:, c0 + dh:c0 + 2 * dh].astype(F32)
        v = ub_ref[:, c0 + 2 * dh:c0 + 3 * dh]
        o_gate = ub_ref[:, c0 + 3 * dh:c0 + 4 * dh]
        fill = head_filler(h)

        fill()
        q = _conv_silu_permuted(q_pre, tail_ref[:, sl], cw_ref[:, sl], cb_ref[:, sl])
        fill()
        k = _conv_silu_permuted(k_pre, tail_ref[:, slk], cw_ref[:, slk], cb_ref[:, slk])
        tail_ref[:, sl] = q_pre[L - tail_rows:, :]
        tail_ref[:, slk] = k_pre[L - tail_rows:, :]
        qb = q.astype(BF16)
        kb = k.astype(BF16)
        fill()

        m_prev = m_ref[h:h + 1, 0:1]
        r_row = grow[h:h + 1, :] - b_row[M_HEADS + h:M_HEADS + h + 1, :]
        bh_col = b_col[:, M_HEADS + h:M_HEADS + h + 1]
        r_col = gcol[:, h:h + 1] - bh_col
        big_m = jnp.maximum(jnp.max(jnp.where(causal, r_row, -jnp.inf), axis=-1, keepdims=True),
                            m_prev)
        decay_w = jnp.exp(jnp.where(causal, (r_row + log_scale) - big_m, -jnp.inf))
        s_inter = jnp.exp(m_prev - big_m)
        floor = _exp_neg(bh_col + big_m)

        w = decay_w * _dot_nt(qb, kb)
        fill()
        num = _dot(w.astype(BF16), v) + s_inter * _dot(qb, c_ref[h].astype(BF16))
        den = (jnp.sum(w, axis=-1, keepdims=True)
               + s_inter * jnp.sum(q * n_ref[h], axis=-1, keepdims=True))
        hh = num / jnp.maximum(jnp.abs(den), floor)
        fill()

        m_end = big_m[L - 1:L, :]
        decay = jnp.exp(m_prev - m_end)
        gk = jnp.exp((r_col + log_scale) - m_end) * k
        c_ref[h] = decay * c_ref[h] + _dot_tn(gk.astype(BF16), v)
        n_ref[h] = decay * n_ref[h] + jnp.sum(gk, axis=0, keepdims=True)
        m_ref[h:h + 1, :] = jnp.broadcast_to(bh_col[L - 1:L, :] + m_end, (1, LANES))
        fill()

        mu = jnp.mean(hh, axis=-1, keepdims=True)
        dlt = hh - mu
        var = jnp.mean(dlt * dlt, axis=-1, keepdims=True)
        hn = dlt * lax.rsqrt(var + HEAD_NORM_EPS) * ng_ref[:, sl]
        outs.append((hn * _sigmoid(o_gate.astype(F32))).astype(BF16))
        after_head(h, fill)
    return jnp.concatenate(outs, axis=-1)


def _block_ref_rows(a, h):
    L, n = a.shape
    if h >= SUBLANES:
        a4 = a.reshape(L // (2 * h), 2 * h, n)
        return jnp.broadcast_to(a4[:, h - 1:h, :], a4.shape).reshape(L, n)
    a3 = a.reshape(L // SUBLANES, SUBLANES, n)
    sub = lax.broadcasted_iota(jnp.int32, a3.shape, 1)
    if h == 4:
        ref = jnp.broadcast_to(a3[:, 3:4, :], a3.shape)
    elif h == 2:
        ref = jnp.where(sub < 4, jnp.broadcast_to(a3[:, 1:2, :], a3.shape),
                        jnp.broadcast_to(a3[:, 5:6, :], a3.shape))
    else:
        ref = jnp.where(sub % 2 == 1, pltpu.roll(a, 1, 0).reshape(a3.shape), a3)
    return ref.reshape(L, n)


def _hgrn_chunk(layer, uh_ref, uhf_ref, lbl_ref, ng_ref, st_ref, hb_ref):
    L = uh_ref.shape[0]
    S = min(H_CHUNK, L)
    levels = [1 << i for i in range(S.bit_length() - 1)]
    hq = uh_ref[:, :H_WIDTH].astype(F32)
    v_all = uh_ref[:, H_WIDTH:2 * H_WIDTH]
    hg = uh_ref[:, 2 * H_WIDTH:].astype(F32)
    hf = uhf_ref[...]

    logits = lbl_ref[...]
    p = jnp.exp(logits - jnp.max(logits, axis=0, keepdims=True))
    p = p / jnp.sum(p, axis=0, keepdims=True)
    lb = jnp.zeros((1, H_WIDTH), F32)
    for d in range(1, layer + 1):
        lb = lb + p[d:d + 1, :]
    softplus_tail = jnp.log(1.0 + _exp_neg(jnp.abs(hf)))
    log1m_lb = jnp.log1p(-lb)
    x2 = log1m_lb + jnp.minimum(hf, 0.0) - softplus_tail
    if layer == 0:
        lf = x2
    else:
        x1 = jnp.log(lb)
        lf = jnp.maximum(x1, x2) + jnp.log(1.0 + _exp_neg(jnp.abs(x1 - x2)))
    kk_all = jnp.exp(log1m_lb - jnp.maximum(hf, 0.0) - softplus_tail)
    qq_all = _silu(hq)
    og_all = _sigmoid(hg) * ng_ref[...]

    rl = lax.broadcasted_iota(jnp.int32, (L, L), 0)
    cl = lax.broadcasted_iota(jnp.int32, (L, L), 1)
    tri = ((cl <= rl) & (rl // S == cl // S)).astype(BF16)
    lf_hi, lf_lo = _split_bf16(lf * (-NEG_LOG2_E))
    a_all = _dot(tri, lf_hi) + _dot(tri, lf_lo)

    D = H_DIM
    row = lax.broadcasted_iota(jnp.int32, (S, 2 * S), 0)
    col = lax.broadcasted_iota(jnp.int32, (S, 2 * S), 1) % S
    trow = lax.broadcasted_iota(jnp.int32, (S, 1), 0)
    upper = [(trow // h) % 2 == 1 for h in levels]
    pair = [(row // (2 * h) == col // (2 * h)) & ((row // h) % 2 == 1) & ((col // h) % 2 == 0)
            for h in levels]
    first = lax.broadcasted_iota(jnp.int32, (S, 2 * S), 1) < S

    def block_diag(x):
        z = jnp.zeros((x.shape[0], D), x.dtype)
        return jnp.concatenate([jnp.concatenate([x[:, :D], z], axis=1),
                                jnp.concatenate([z, x[:, D:]], axis=1)], axis=0)

    def block_diag_cols(xt):
        z = jnp.zeros((D, xt.shape[1]), xt.dtype)
        return jnp.concatenate([jnp.concatenate([xt[:D], z], axis=1),
                                jnp.concatenate([z, xt[D:]], axis=1)], axis=0)

    def per_head(fn, x):
        return jnp.concatenate([fn(x[:, :D]), fn(x[:, D:])], axis=1)

    def unit(sc, gp):
        rs = slice(sc * S, (sc + 1) * S)
        sl = slice(2 * gp * D, (2 * gp + 2) * D)
        a = a_all[rs, sl]
        kk = kk_all[rs, sl]
        qq = qq_all[rs, sl]
        vb = v_all[rs, sl]

        qk = qq * kk
        diag = jnp.where(first, jnp.sum(qk[:, :D], axis=-1, keepdims=True),
                         jnp.sum(qk[:, D:], axis=-1, keepdims=True))
        att = jnp.where(row == col, diag, 0.0)
        for h, up, pr in zip(levels, upper, pair):
            fac = jnp.exp2(_neg_abs(a - _block_ref_rows(a, h)))
            pm = (jnp.where(up, qq, kk) * fac).astype(BF16)
            att = jnp.where(pr, _dot(pm, block_diag_cols(pm.T)), att)

        st = jnp.concatenate([st_ref[2 * gp], st_ref[2 * gp + 1]], axis=1)
        out = (_dot(att.astype(BF16), block_diag(vb))
               + _dot_nt((qq * jnp.exp2(a)).astype(BF16), block_diag(st.astype(BF16))))
        a_end = a[S - 1:S, :]
        upd = _dot_tn(vb, (kk * jnp.exp2(a_end - a)).astype(BF16))
        st_new = jnp.exp2(a_end) * st + jnp.concatenate([upd[:D, :D], upd[D:, D:]], axis=1)
        st_ref[2 * gp] = st_new[:, :D]
        st_ref[2 * gp + 1] = st_new[:, D:]

        ms = per_head(lambda o: jnp.broadcast_to(jnp.mean(o * o, axis=-1, keepdims=True), o.shape), out)
        hb_ref[rs, sl] = (out * lax.rsqrt(ms + HEAD_NORM_EPS) * og_all[rs, sl]).astype(hb_ref.dtype)

    return unit, [(sc, gp) for sc in range(L // S) for gp in range(H_HEADS // 2)]


def _mixer_kernel(layer, n_chunks, xcur_ref, xnext_ref, wb_ref, bb_ref, wf_ref, bf_ref, wgt_ref, bgt_ref,
                  cw_ref, cb_ref, mng_ref, lbl_ref, hng_ref, ha_ref, hb_ref,
                  ub_ref, ug_ref, gr_ref, uh_ref, uhf_ref, c_ref, n_ref, m_ref, tail_ref, st_ref):
    L = xcur_ref.shape[0]
    nv = L // SUBLANES
    k = pl.program_id(0)
    rr = lax.broadcasted_iota(jnp.int32, (L, L), 0)
    cc = lax.broadcasted_iota(jnp.int32, (L, L), 1)
    t_r = (rr % SUBLANES) * nv + rr // SUBLANES
    t_c = (cc % SUBLANES) * nv + cc // SUBLANES
    to_permuted = lambda x: _dot((cc == t_r).astype(BF16), x).astype(BF16)

    def add_rows(acc, b8):
        n = acc.shape[1]
        return (acc.reshape(L // SUBLANES, SUBLANES, n) + b8[None]).reshape(L, n)

    def gate_pieces(x_perm):
        def gates():
            ug_ref[...] = add_rows(_dot(x_perm, wf_ref[...]), bf_ref[...])
            gr_ref[...] = _dot_nt(wgt_ref[...], x_perm) + bgt_ref[...]
        return [gates]

    def head_pieces(x_perm, h):
        def piece(c):
            cs = slice(c, c + PIECE_COLS)
            ub_ref[:, cs] = add_rows(_dot(x_perm, wb_ref[:, cs]), bb_ref[:, cs]).astype(BF16)
        return [functools.partial(piece, c)
                for c in range(h * HEAD_COLS, (h + 1) * HEAD_COLS, PIECE_COLS)]

    def hgrn_pieces(x_nat):
        def piece(c):
            cs = slice(COL_HQ + c, COL_HQ + c + PIECE_COLS)
            acc = add_rows(_dot(x_nat, wb_ref[:, cs]), bb_ref[:, cs])
            if c < 3 * H_WIDTH:
                uh_ref[:, c:c + PIECE_COLS] = acc.astype(BF16)
            else:
                uhf_ref[:, c - 3 * H_WIDTH:c - 3 * H_WIDTH + PIECE_COLS] = acc
        return [functools.partial(piece, c) for c in range(0, 4 * H_WIDTH, PIECE_COLS)]

    def filler(pieces):
        it = iter(pieces)

        def fill(drain=False):
            for thunk in it:
                thunk()
                if not drain:
                    break
        return fill

    @pl.when(k == 0)
    def _():
        x0 = xcur_ref[...]
        xp0 = to_permuted(x0)
        for thunk in (gate_pieces(xp0) + [p for h in range(M_HEADS) for p in head_pieces(xp0, h)]
                      + hgrn_pieces(x0)):
            thunk()

    @pl.when(lax.rem(k, n_chunks) == 0)
    def _():
        c_ref[...] = jnp.zeros_like(c_ref)
        n_ref[...] = jnp.zeros_like(n_ref)
        m_ref[...] = jnp.zeros_like(m_ref)
        tail_ref[...] = jnp.zeros_like(tail_ref)
        st_ref[...] = jnp.zeros_like(st_ref)

    xn = xnext_ref[...]
    xpn = to_permuted(xn)
    gcol = ug_ref[...]
    grow = gr_ref[...]
    filler(gate_pieces(xpn))(drain=True)
    causal = t_c <= t_r
    tri = causal.astype(BF16)
    tri_t = (t_r <= t_c).astype(BF16)
    lf_hi, lf_lo = _split_bf16(_log_sigmoid(gcol))
    b_col = _dot(tri, lf_hi) + _dot(tri, lf_lo)
    lr_hi, lr_lo = _split_bf16(_log_sigmoid(grow))
    b_row = _dot(lr_hi, tri_t) + _dot(lr_lo, tri_t)

    hgrn_unit, units = _hgrn_chunk(layer, uh_ref, uhf_ref, lbl_ref, hng_ref, st_ref, hb_ref)
    units_per_head = -(-len(units) // M_HEADS)
    next_hgrn = hgrn_pieces(xn)
    hgrn_per_head = -(-len(next_hgrn) // M_HEADS)

    def head_filler(h):
        mine = head_pieces(xpn, h)
        ride = next_hgrn[h * hgrn_per_head:(h + 1) * hgrn_per_head]
        mixed = [p for pair in zip(mine, ride) for p in pair] + mine[len(ride):] + ride[len(mine):]
        return filler(mixed)

    def after_head(h, fill):
        for sc, g in units[h * units_per_head:(h + 1) * units_per_head]:
            fill()
            fill()
            hgrn_unit(sc, g)
        fill(drain=True)

    h_perm = _mlstm_chunk(ub_ref, causal, gcol, grow, b_col, b_row, cw_ref, cb_ref, mng_ref,
                          c_ref, n_ref, m_ref, tail_ref, head_filler, after_head)
    ha_ref[...] = _dot((rr == t_c).astype(BF16), h_perm).astype(ha_ref.dtype)


def _mixer(layer, xb3, wb, bb, wf, bf, wgt, bgt, conv_w, conv_b, m_norm_g, lb_logits, h_norm_g):
    bsz, s, d = xb3.shape
    L = min(MIX_CHUNK, s)
    assert s % L == 0 and L % min(H_CHUNK, L) == 0 and L % (2 * SUBLANES) == 0, (s, L)
    n_chunks = s // L
    steps = bsz * n_chunks
    xk = xb3.reshape(steps, L, d)
    resident = lambda a: pl.BlockSpec(a.shape, lambda k: (0,) * a.ndim, pipeline_mode=pl.Buffered(1))
    rows = lambda w: pl.BlockSpec((None, L, w), lambda k: (k, 0, 0))
    next_rows = pl.BlockSpec((None, L, d), lambda k: (jnp.minimum(k + 1, steps - 1), 0, 0))
    consts = (wb, bb, wf, bf, wgt, bgt, conv_w, conv_b, m_norm_g, lb_logits, h_norm_g)
    return pl.pallas_call(
        functools.partial(_mixer_kernel, layer, n_chunks),
        grid=(steps,),
        in_specs=[rows(d), next_rows] + [resident(a) for a in consts],
        out_specs=[rows(M_WIDTH), rows(H_WIDTH)],
        out_shape=[jax.ShapeDtypeStruct((steps, L, M_WIDTH), BF16),
                   jax.ShapeDtypeStruct((steps, L, H_WIDTH), BF16)],
        scratch_shapes=[pltpu.VMEM((L, COL_HQ), BF16),
                        pltpu.VMEM((L, LANES), F32),
                        pltpu.VMEM((SUBLANES, L), F32),
                        pltpu.VMEM((L, 3 * H_WIDTH), BF16),
                        pltpu.VMEM((L, H_WIDTH), F32),
                        pltpu.VMEM((M_HEADS, M_HEAD_DIM, M_HEAD_DIM), F32),
                        pltpu.VMEM((M_HEADS, 1, M_HEAD_DIM), F32),
                        pltpu.VMEM((SUBLANES, LANES), F32),
                        pltpu.VMEM(((CONV_WIDTH - 1) * SUBLANES, 2 * M_WIDTH), F32),
                        pltpu.VMEM((H_HEADS, H_DIM, H_DIM), F32)],
        compiler_params=_params("arbitrary"),
        name="mixer",
    )(xk, xk, *consts)


def _tail_kernel(alpha, ha_ref, hb_ref, x_ref, xb_ref, wgab_ref, bgab_ref, pa_ref, pb_ref, wo_ref,
                 g1_ref, b1_ref, wg_ref, wu_ref, wd_ref, g2_ref, b2_ref, o_ref, ob_ref):
    gates = _sigmoid(_dot(xb_ref[...], wgab_ref[...]) + bgab_ref[...])
    d = x_ref.shape[1]
    mixed = gates[:, :d] * _dot(ha_ref[...], pa_ref[...]) + gates[:, d:] * _dot(hb_ref[...], pb_ref[...])
    y1 = alpha * x_ref[...] + _dot(mixed.astype(BF16), wo_ref[...])
    x1 = _layernorm_rows(y1, g1_ref[...], b1_ref[...])
    x1b = x1.astype(BF16)
    hid = (_silu(_dot(x1b, wg_ref[...])) * _dot(x1b, wu_ref[...])).astype(BF16)
    y2 = alpha * x1 + _dot(hid, wd_ref[...])
    out = _layernorm_rows(y2, g2_ref[...], b2_ref[...])
    o_ref[...] = out
    ob_ref[...] = out.astype(BF16)


def _tail(alpha, ha, hb, x, xb, wgab, bgab, pa, pb, wo, g1, b1, wg, wu, wd, g2, b2):
    t, d = x.shape
    tm = min(TAIL_TM, t)
    assert t % tm == 0, (t, tm)
    rows = lambda w: pl.BlockSpec((tm, w), lambda i: (i, 0))
    full = lambda a: pl.BlockSpec(a.shape, lambda i: (0,) * a.ndim, pipeline_mode=pl.Buffered(1))
    consts = (wgab, bgab, pa, pb, wo, g1, b1, wg, wu, wd, g2, b2)
    return pl.pallas_call(
        functools.partial(_tail_kernel, alpha),
        grid=(t // tm,),
        in_specs=[rows(M_WIDTH), rows(H_WIDTH), rows(d), rows(d)] + [full(a) for a in consts],
        out_specs=[rows(d), rows(d)],
        out_shape=[jax.ShapeDtypeStruct((t, d), F32), jax.ShapeDtypeStruct((t, d), BF16)],
        compiler_params=_params("parallel"),
        name="merge_ffn_ln",
    )(ha, hb, x, xb, *consts)


def _in_proj_layout(w_in, b_in):
    sizes = (M_WIDTH,) * 4 + (M_HEADS, M_HEADS) + (H_WIDTH,) * 6
    off = [int(o) for o in np.cumsum((0,) + sizes)]
    seg = lambda a, i: a[..., off[i]:off[i + 1]]
    rows8 = lambda b: jnp.broadcast_to(b[:, None, :], (b.shape[0], SUBLANES, b.shape[1]))
    dh = M_HEAD_DIM

    def main_group(a):
        heads = [seg(a, i)[..., h * dh:(h + 1) * dh] for h in range(M_HEADS) for i in (0, 1, 2, 3)]
        return jnp.concatenate(heads + [seg(a, i) for i in (6, 8, 9, 7)], axis=-1)

    wb = main_group(w_in).astype(BF16)
    bb = rows8(main_group(b_in))
    wgab = jnp.concatenate([seg(w_in, 10), seg(w_in, 11)], axis=-1).astype(BF16)
    bgab = jnp.concatenate([seg(b_in, 10), seg(b_in, 11)], axis=-1)[:, None, :]
    pad = LANES - 2 * M_HEADS
    wf = jnp.concatenate([seg(w_in, 4), seg(w_in, 5),
                          jnp.zeros(w_in.shape[:2] + (pad,), w_in.dtype)], axis=-1).astype(BF16)
    bf = rows8(jnp.concatenate([seg(b_in, 4), seg(b_in, 5),
                                jnp.zeros(b_in.shape[:1] + (pad,), b_in.dtype)], axis=-1))
    wgt = jnp.swapaxes(jnp.concatenate([seg(w_in, 4), seg(w_in, 5)], axis=-1), 1, 2).astype(BF16)
    bgt = jnp.concatenate([seg(b_in, 4), seg(b_in, 5)], axis=-1)[:, :, None]
    return wb, bb, wf, bf, wgt, bgt, wgab, bgab


def kernel(x, w_in, b_in, conv_w, conv_b, m_norm_g, lb_logits, h_norm_g, w_proj_a, w_proj_b, w_out,
           ln1_g, ln1_b, w_ffn_gate, w_ffn_up, w_ffn_down, ln2_g, ln2_b):
    bsz, s, d = x.shape
    depth = w_in.shape[0]
    alpha = float((2 * depth) ** 0.25)
    t = bsz * s

    wb, bb, wf, bf, wgt, bgt, wgab, bgab = _in_proj_layout(w_in, b_in)
    pa = w_proj_a.astype(BF16)
    pb = w_proj_b.astype(BF16)
    wo = w_out.astype(BF16)
    wg = w_ffn_gate.astype(BF16)
    wu = w_ffn_up.astype(BF16)
    wd = w_ffn_down.astype(BF16)

    xf = x.reshape(t, d)
    xb = xf.astype(BF16)
    for layer in range(depth):
        ha, hb = _mixer(layer, xb.reshape(bsz, s, d), wb[layer], bb[layer], wf[layer], bf[layer],
                        wgt[layer], bgt[layer], conv_w[layer], conv_b[layer][None, :],
                        m_norm_g[layer][None, :], lb_logits, h_norm_g[layer][None, :])
        xf, xb = _tail(alpha, ha.reshape(t, M_WIDTH), hb.reshape(t, H_WIDTH), xf, xb,
                       wgab[layer], bgab[layer], pa[layer], pb[layer], wo[layer],
                       ln1_g[layer][None, :], ln1_b[layer][None, :],
                       wg[layer], wu[layer], wd[layer], ln2_g[layer][None, :], ln2_b[layer][None, :])
    return xf.reshape(bsz, s, d)
```
